```python
import math
import jax, jax.numpy as jnp
from jax import lax
import numpy as np

D_MODEL = 1024
BATCH = 4
SEQ = 4096
DEPTH = 1
DEC_BATCH = 128
DEC_SEQ = 8
PAST_LEN = 8192
PAGE_SIZE = 128

HEAD_DIM = 64
N_RET_HEADS = 8
N_SB_HEADS = 8
RET_WIDTH = N_RET_HEADS * HEAD_DIM
SB_WIDTH = N_SB_HEADS * HEAD_DIM
MIX_WIDTH = RET_WIDTH + SB_WIDTH
IN_WIDTH = 4 * RET_WIDTH + 3 * SB_WIDTH
D_FF = 4 * D_MODEL
RET_CHUNK = 128
SB_QBLOCK = 128
SB_BIAS_INIT = -5.0
ROPE_BASE = 10000.0
NORM_EPS = 1e-6

kernel_name = "hymba_retention_stickbreaking_step"


def rms_norm(x, g):
    xf = x.astype(jnp.float32)
    y = xf * lax.rsqrt(jnp.mean(xf * xf, axis=-1, keepdims=True) + NORM_EPS)
    return (y * g.astype(jnp.float32)).astype(x.dtype)


def head_norm(o, g, center):
    if center:
        o = o - jnp.mean(o, axis=-1, keepdims=True)
    y = o * lax.rsqrt(jnp.mean(o * o, axis=-1, keepdims=True) + NORM_EPS)
    b, l = y.shape[0], y.shape[1]
    return y.reshape(b, l, -1) * g.astype(jnp.float32)


def retention_log_decays():
    return jnp.log1p(-jnp.exp2(-5.0 - jnp.arange(N_RET_HEADS, dtype=jnp.float32)))


def rotary(x, pos):
    half = HEAD_DIM // 2
    inv = ROPE_BASE ** (-jnp.arange(half, dtype=jnp.float32) / half)
    ang = pos[:, None] * inv[None, :]
    cos = jnp.cos(ang)[None, :, None, :]
    sin = jnp.sin(ang)[None, :, None, :]
    x1, x2 = x[..., :half], x[..., half:]
    return jnp.concatenate([x1 * cos - x2 * sin, x1 * sin + x2 * cos], axis=-1)


def retention_chunk(q, k, v, state, log_gamma):
    L = q.shape[1]
    idx = jnp.arange(L, dtype=jnp.float32)
    diff = idx[:, None] - idx[None, :]
    decay = jnp.where(diff >= 0, jnp.exp(log_gamma[:, None, None] * jnp.maximum(diff, 0.0)), 0.0)
    scores = jnp.einsum('bihd,bjhd->bhij', q, k) * decay[None]
    intra = jnp.einsum('bhij,bjhe->bihe', scores, v)
    q_decay = jnp.exp(log_gamma[None, :] * (idx[:, None] + 1.0))
    cross = jnp.einsum('bihd,bhde->bihe', q, state) * q_decay[None, :, :, None]
    k_decay = jnp.exp(log_gamma[None, :] * (L - 1.0 - idx[:, None]))
    new_state = (jnp.exp(log_gamma * L)[None, :, None, None] * state
                 + jnp.einsum('bjhd,jh,bjhe->bhde', k, k_decay, v))
    return intra + cross, new_state


def retention_prompt(q, k, v, log_gamma):
    B, L, H, d = q.shape
    n = L // RET_CHUNK

    def to_chunks(a):
        return a.reshape(B, n, RET_CHUNK, H, d).transpose(1, 0, 2, 3, 4)

    def step(state, qkv):
        qc, kc, vc = qkv
        o, state = retention_chunk(qc, kc, vc, state, log_gamma)
        return state, o

    s0 = jnp.zeros((B, H, d, d), jnp.float32)
    s_fin, o = lax.scan(step, s0, (to_chunks(q), to_chunks(k), to_chunks(v)))
    return o.transpose(1, 0, 2, 3, 4).reshape(B, L, H, d), s_fin


def sb_weights(z, mask):
    log_keep = jnp.where(mask, jax.nn.log_sigmoid(-z), 0.0)
    later = lax.cumsum(log_keep, axis=z.ndim - 1, reverse=True) - log_keep
    return jnp.where(mask, jnp.exp(jax.nn.log_sigmoid(z) + later), 0.0)


def sb_prompt(q, k, v, bias):
    B, L, H, d = q.shape
    n = L // SB_QBLOCK
    scale = HEAD_DIM ** -0.5
    qb = q.reshape(B, n, SB_QBLOCK, H, d).transpose(1, 0, 2, 3, 4)
    qpos = jnp.arange(L).reshape(n, SB_QBLOCK)
    kpos = jnp.arange(L)
    b4 = bias[None, :, None, None]

    def block(args):
        qblk, pos = args
        z = jnp.einsum('bqhd,bkhd->bhqk', qblk, k) * scale + b4
        a = sb_weights(z, kpos[None, :] < pos[:, None])
        return jnp.einsum('bhqk,bkhd->bqhd', a, v)

    o = lax.map(block, (qb, qpos))
    return o.transpose(1, 0, 2, 3, 4).reshape(B, L, H, d)


def sb_sample(q, k_new, v_new, k_pages, v_pages, bias):
    DB, T, H, d = q.shape
    scale = HEAD_DIM ** -0.5
    kp = k_pages.reshape(DB, -1, H, d).astype(jnp.float32)
    vp = v_pages.reshape(DB, -1, H, d).astype(jnp.float32)
    P = kp.shape[1]
    z = jnp.concatenate([jnp.einsum('bqhd,bkhd->bhqk', q, kp),
                         jnp.einsum('bqhd,bkhd->bhqk', q, k_new)], axis=-1) * scale + bias[None, :, None, None]
    kpos = jnp.arange(P + T)
    qpos = P + jnp.arange(T)
    a = sb_weights(z, kpos[None, :] < qpos[:, None])
    return (jnp.einsum('bhqk,bkhd->bqhd', a[..., :P], vp)
            + jnp.einsum('bhqk,bkhd->bqhd', a[..., P:], v_new))


def mixer_inputs(h, w_in, pos):
    p = jnp.einsum('bld,de->ble', h, w_in).astype(jnp.float32)
    R, S = RET_WIDTH, SB_WIDTH
    rq, rk, rv, rg, sq, sk, sv = jnp.split(p, [R, 2 * R, 3 * R, 4 * R, 4 * R + S, 4 * R + 2 * S], axis=-1)
    b, l = h.shape[0], h.shape[1]
    heads_r = lambda a: a.reshape(b, l, N_RET_HEADS, HEAD_DIM)
    heads_s = lambda a: a.reshape(b, l, N_SB_HEADS, HEAD_DIM)
    rq = rotary(heads_r(rq), pos)
    rk = rotary(heads_r(rk), pos) * (HEAD_DIM ** -0.5)
    return rq, rk, heads_r(rv), rg, heads_s(sq), heads_s(sk), heads_s(sv)


def mixer_output(ret_o, gate, sb_o, ret_norm_g, sb_norm_g, w_out, dtype):
    ret = head_norm(ret_o, ret_norm_g, True) * jax.nn.silu(gate)
    sb = head_norm(sb_o, sb_norm_g, False)
    merged = jnp.concatenate([ret, sb], axis=-1).astype(dtype)
    return jnp.einsum('ble,ed->bld', merged, w_out)


def channel_mlp(h, w_up, w_down):
    u = jax.nn.relu(jnp.einsum('bld,df->blf', h, w_up))
    return jnp.einsum('blf,fd->bld', u * u, w_down)


def setup_inputs(seed: int = 0) -> dict:
    key = jax.random.key(seed)
    ks = jax.random.split(key, 20)
    n_pages = PAST_LEN // PAGE_SIZE
    n_phys = (DEC_BATCH * n_pages * 5) // 4
    f32 = jnp.float32
    nrm = lambda k, s: jax.random.normal(k, s, f32)
    gain = lambda k, s: 1.0 + 0.05 * nrm(k, s)
    page_table = jax.random.permutation(ks[4], n_phys)[:DEC_BATCH * n_pages].reshape(DEC_BATCH, n_pages).astype(jnp.int32)
    return {
        "x_prompt": nrm(ks[0], (BATCH, SEQ, D_MODEL)),
        "x_sample": nrm(ks[1], (DEC_BATCH, DEC_SEQ, D_MODEL)),
        "cache_sb_k": nrm(ks[2], (DEPTH, n_phys, PAGE_SIZE, N_SB_HEADS, HEAD_DIM)),
        "cache_sb_v": nrm(ks[3], (DEPTH, n_phys, PAGE_SIZE, N_SB_HEADS, HEAD_DIM)),
        "page_table": page_table,
        "state_ret": 0.3 * nrm(ks[5], (DEPTH, DEC_BATCH, N_RET_HEADS, HEAD_DIM, HEAD_DIM)),
        "norm_mix_pre": gain(ks[6], (DEPTH, D_MODEL)),
        "norm_mix_post": gain(ks[7], (DEPTH, D_MODEL)),
        "w_in": nrm(ks[8], (DEPTH, D_MODEL, IN_WIDTH)) * D_MODEL ** -0.5,
        "ret_norm_g": gain(ks[9], (DEPTH, RET_WIDTH)),
        "sb_bias": SB_BIAS_INIT + 0.1 * nrm(ks[16], (DEPTH, N_SB_HEADS)),
        "sb_norm_g": gain(ks[10], (DEPTH, SB_WIDTH)),
        "w_out": nrm(ks[11], (DEPTH, MIX_WIDTH, D_MODEL)) * MIX_WIDTH ** -0.5,
        "norm_mlp_pre": gain(ks[12], (DEPTH, D_MODEL)),
        "norm_mlp_post": gain(ks[13], (DEPTH, D_MODEL)),
        "w_up": nrm(ks[14], (DEPTH, D_MODEL, D_FF)) * D_MODEL ** -0.5,
        "w_down": nrm(ks[15], (DEPTH, D_FF, D_MODEL)) * D_FF ** -0.5,
    }


def reference(x_prompt, x_sample, cache_sb_k, cache_sb_v, page_table, state_ret,
              norm_mix_pre, norm_mix_post, w_in, ret_norm_g, sb_bias, sb_norm_g, w_out,
              norm_mlp_pre, norm_mlp_post, w_up, w_down):
    log_gamma = retention_log_decays()
    pos_p = jnp.arange(x_prompt.shape[1], dtype=jnp.float32)
    pos_s = PAST_LEN + jnp.arange(x_sample.shape[1], dtype=jnp.float32)
    y_p, y_s = x_prompt, x_sample
    k_p_l, v_p_l, r_p_l, k_s_l, v_s_l, r_s_l = [], [], [], [], [], []
    for l in range(DEPTH):
        bias = sb_bias[l].astype(jnp.float32)
        h = rms_norm(y_p, norm_mix_pre[l])
        rq, rk, rv, rg, sq, sk, sv = mixer_inputs(h, w_in[l], pos_p)
        ret_o, s_p = retention_prompt(rq, rk, rv, log_gamma)
        sb_o = sb_prompt(sq, sk, sv, bias)
        mix = mixer_output(ret_o, rg, sb_o, ret_norm_g[l], sb_norm_g[l], w_out[l], y_p.dtype)
        y_p = y_p + rms_norm(mix, norm_mix_post[l])
        y_p = y_p + rms_norm(channel_mlp(rms_norm(y_p, norm_mlp_pre[l]), w_up[l], w_down[l]), norm_mlp_post[l])
        k_p_l.append(sk.astype(x_prompt.dtype))
        v_p_l.append(sv.astype(x_prompt.dtype))
        r_p_l.append(s_p.astype(x_prompt.dtype))
        h = rms_norm(y_s, norm_mix_pre[l])
        rq, rk, rv, rg, sq, sk, sv = mixer_inputs(h, w_in[l], pos_s)
        ret_o, s_s = retention_chunk(rq, rk, rv, state_ret[l].astype(jnp.float32), log_gamma)
        k_pages = cache_sb_k[l][page_table]
        v_pages = cache_sb_v[l][page_table]
        sb_o = sb_sample(sq, sk, sv, k_pages, v_pages, bias)
        mix = mixer_output(ret_o, rg, sb_o, ret_norm_g[l], sb_norm_g[l], w_out[l], y_s.dtype)
        y_s = y_s + rms_norm(mix, norm_mix_post[l])
        y_s = y_s + rms_norm(channel_mlp(rms_norm(y_s, norm_mlp_pre[l]), w_up[l], w_down[l]), norm_mlp_post[l])
        k_s_l.append(sk.astype(cache_sb_k.dtype))
        v_s_l.append(sv.astype(cache_sb_v.dtype))
        r_s_l.append(s_s.astype(state_ret.dtype))
    k_prompt = jnp.stack(k_p_l)
    v_prompt = jnp.stack(v_p_l)
    ret_prompt = jnp.stack(r_p_l)
    k_sample = jnp.stack(k_s_l)
    v_sample = jnp.stack(v_s_l)
    ret_sample = jnp.stack(r_s_l)
    return (y_p, y_s, k_prompt, v_prompt, ret_prompt, k_sample, v_sample, ret_sample)
```

```python
import functools

import numpy as np
import jax
import jax.numpy as jnp
from jax import lax
from jax.experimental import pallas as pl
from jax.experimental.pallas import tpu as pltpu

F32 = jnp.float32
BF16 = jnp.bfloat16

D_MODEL = 1024
HEAD_DIM = 64
N_HEADS = 8
WIDTH = N_HEADS * HEAD_DIM
N_SEG = 7
D_FF = 4 * D_MODEL
RET_CHUNK = 128
PAGE_SIZE = 128
ROPE_BASE = 10000.0
NORM_EPS = 1e-6
QK_SCALE = HEAD_DIM ** -0.5

SB_TILE = 256
SB_GROUPS = 8
SB_GROUP_LEN = SB_TILE // SB_GROUPS
PAGES_PER_STEP = 8
SAMPLE_SCAN_BLOCK = 256

_LOG_GAMMA = [float(np.log1p(-np.exp2(np.float32(-5.0 - h)), dtype=np.float32)) for h in range(N_HEADS)]

_VMEM_LIMIT = 56 * 1024 * 1024


def _params(semantics):
    return pltpu.CompilerParams(dimension_semantics=semantics, vmem_limit_bytes=_VMEM_LIMIT)


def _softplus(z):
    return jnp.maximum(z, 0.0) + jnp.log1p(jnp.exp(-jnp.abs(z)))


def _proj_kernel(x_ref, g_ref, w_ref, cos_ref, sin_ref,
                 rq_ref, rk_ref, rv_ref, gate_ref, sq_ref, sk_ref, sv_ref):
    x = x_ref[...]
    h = x * lax.rsqrt(jnp.mean(x * x, axis=-1, keepdims=True) + NORM_EPS) * g_ref[...]
    hb = h.astype(BF16)

    def seg(i):
        return jnp.dot(hb, w_ref[:, i * WIDTH:(i + 1) * WIDTH], preferred_element_type=F32)

    cos = jnp.concatenate([cos_ref[...]] * 4, axis=1)
    sin = jnp.concatenate([sin_ref[...]] * 4, axis=1)
    lane = lax.broadcasted_iota(jnp.int32, cos.shape, 1)
    first_half = (lane % HEAD_DIM) < (HEAD_DIM // 2)

    def rotary(p):
        swapped = jnp.where(first_half, pltpu.roll(p, WIDTH - HEAD_DIM // 2, 1), pltpu.roll(p, HEAD_DIM // 2, 1))
        return p * cos + swapped * sin

    rq_ref[...] = rotary(seg(0))
    rk_ref[...] = rotary(seg(1)) * QK_SCALE
    rv_ref[...] = seg(2)
    gate_ref[...] = seg(3)
    sq_ref[...] = (seg(4) * QK_SCALE).astype(BF16)
    sk_ref[...] = seg(5)
    sv_ref[...] = seg(6)


def _proj(x, g, w_bf16, cos_tab, sin_tab, tm):
    t = x.shape[0]
    n_tab = cos_tab.shape[0] // tm
    tok = lambda i: (i, 0)
    tab = lambda i: (i % n_tab, 0)
    const = lambda i: (0, 0)
    out_f32 = jax.ShapeDtypeStruct((t, WIDTH), F32)
    out_bf16 = jax.ShapeDtypeStruct((t, WIDTH), BF16)
    blk = pl.BlockSpec((tm, WIDTH), tok)
    return pl.pallas_call(
        _proj_kernel,
        grid=(t // tm,),
        in_specs=[pl.BlockSpec((tm, D_MODEL), tok),
                  pl.BlockSpec((1, D_MODEL), const),
                  pl.BlockSpec((D_MODEL, N_SEG * WIDTH), const),
                  pl.BlockSpec((tm, 128), tab),
                  pl.BlockSpec((tm, 128), tab)],
        out_specs=[blk] * 7,
        out_shape=[out_f32, out_f32, out_f32, out_f32, out_bf16, out_f32, out_f32],
        compiler_params=_params(("parallel",)),
        name="proj",
    )(x, g, w_bf16, cos_tab, sin_tab)


def _ret_kernel(q_ref, k_ref, v_ref, gate_ref, s0_ref, g_ref, o_ref, s_ref, *, chunk):
    @pl.when(pl.program_id(1) == 0)
    def _():
        s_ref[...] = s0_ref[...]

    row = lax.broadcasted_iota(jnp.int32, (chunk, chunk), 0)
    col = lax.broadcasted_iota(jnp.int32, (chunk, chunk), 1)
    diff = (row - col).astype(F32)
    idx = lax.broadcasted_iota(jnp.int32, (chunk, 1), 0).astype(F32)
    nt = (((1,), (1,)), ((), ()))
    tn = (((0,), (0,)), ((), ()))
    for h in range(N_HEADS):
        lg = _LOG_GAMMA[h]
        sl = slice(h * HEAD_DIM, (h + 1) * HEAD_DIM)
        qh = q_ref[:, sl].astype(BF16)
        kf = k_ref[:, sl]
        vh = v_ref[:, sl].astype(BF16)
        decay = jnp.where(diff >= 0, jnp.exp(lg * jnp.maximum(diff, 0.0)), 0.0)
        scores = lax.dot_general(qh, kf.astype(BF16), nt, preferred_element_type=F32) * decay
        intra = jnp.dot(scores.astype(BF16), vh, preferred_element_type=F32)
        state = s_ref[0, h]
        cross = jnp.dot(qh, state.astype(BF16), preferred_element_type=F32) * jnp.exp(lg * (idx + 1.0))
        o = intra + cross
        kd = (kf * jnp.exp(lg * (chunk - 1.0 - idx))).astype(BF16)
        s_ref[0, h] = (jnp.exp(jnp.full((1, 1), lg * chunk, F32)) * state
                       + lax.dot_general(kd, vh, tn, preferred_element_type=F32))
        o = o - jnp.mean(o, axis=-1, keepdims=True)
        y = o * lax.rsqrt(jnp.mean(o * o, axis=-1, keepdims=True) + NORM_EPS)
        gate = gate_ref[:, sl]
        o_ref[:, sl] = y * g_ref[:, sl] * (gate / (1.0 + jnp.exp(-gate)))


def _retention(rq, rk, rv, gate, state0, g, n_seq, chunk):
    t = rq.shape[0]
    n_chunks = t // (n_seq * chunk)
    tok = lambda b, c: (b * n_chunks + c, 0)
    st = lambda b, c: (b, 0, 0, 0)
    blk = pl.BlockSpec((chunk, WIDTH), tok)
    st_blk = pl.BlockSpec((1, N_HEADS, HEAD_DIM, HEAD_DIM), st)
    return pl.pallas_call(
        functools.partial(_ret_kernel, chunk=chunk),
        grid=(n_seq, n_chunks),
        in_specs=[blk, blk, blk, blk, st_blk, pl.BlockSpec((1, WIDTH), lambda b, c: (0, 0))],
        out_specs=[blk, st_blk],
        out_shape=[jax.ShapeDtypeStruct((t, WIDTH), F32),
                   jax.ShapeDtypeStruct((n_seq, N_HEADS, HEAD_DIM, HEAD_DIM), F32)],
        compiler_params=_params(("parallel", "arbitrary")),
        name="retention",
    )(rq, rk, rv, gate, state0, g)


def _sbp_kernel(bias_ref, qt_ref, k_ref, vt_ref, o_ref):
    n_tiles = qt_ref.shape[2]
    bias = bias_ref[pl.program_id(1)]
    grp = lax.broadcasted_iota(jnp.int32, (SB_GROUPS, SB_TILE), 0)
    qry = lax.broadcasted_iota(jnp.int32, (SB_GROUPS, SB_TILE), 1)
    diag_bound = qry - SB_GROUP_LEN * grp

    def tile(c, qt, carry, acc, masked):
        s = jnp.dot(k_ref[0, 0, c], qt, preferred_element_type=F32)
        run = jnp.zeros((SB_GROUPS, SB_TILE), F32)
        logits = [None] * SB_GROUP_LEN
        for j in reversed(range(SB_GROUP_LEN)):
            z = s[SB_GROUPS * j:SB_GROUPS * (j + 1), :] + bias
            sp = _softplus(z)
            if masked:
                sp = jnp.where(diag_bound > j, sp, 0.0)
            run = run - sp
            logits[j] = z + run
        total = run
        scan = total
        for sh in (1, 2, 4):
            scan = scan + jnp.where(grp + sh < SB_GROUPS, pltpu.roll(scan, SB_GROUPS - sh, 0), 0.0)
        offset = scan - total + carry
        carry = jnp.broadcast_to(scan[0:1, :], carry.shape) + carry
        slabs = []
        for j in range(SB_GROUP_LEN):
            a = jnp.exp(logits[j] + offset)
            if masked:
                a = jnp.where(diag_bound > j, a, 0.0)
            slabs.append(a)
        a = jnp.concatenate(slabs, axis=0).astype(BF16)
        acc = acc + jnp.dot(vt_ref[0, 0, c], a, preferred_element_type=F32)
        return carry, acc

    def qblock(qb, _):
        qt = qt_ref[0, 0, qb]
        zero_c = jnp.zeros((SB_GROUPS, SB_TILE), F32)
        zero_a = jnp.zeros((HEAD_DIM, SB_TILE), F32)
        carry, acc = tile(qb, qt, zero_c, zero_a, True)

        def body(i, ca):
            return tile(qb - 1 - i, qt, ca[0], ca[1], False)

        carry, acc = lax.fori_loop(0, qb, body, (carry, acc))
        o_ref[0, 0, qb] = acc
        return 0

    lax.fori_loop(0, n_tiles, qblock, 0)


def _sb_prompt(bias, qt, kp, vt):
    b, h, n_tiles = qt.shape[:3]
    idx = lambda i, j: (i, j, 0, 0, 0)
    t_blk = pl.BlockSpec((1, 1, n_tiles, HEAD_DIM, SB_TILE), idx)
    k_blk = pl.BlockSpec((1, 1, n_tiles, SB_TILE, HEAD_DIM), idx)
    return pl.pallas_call(
        _sbp_kernel,
        grid=(b, h),
        in_specs=[pl.BlockSpec(memory_space=pltpu.SMEM), t_blk, k_blk, t_blk],
        out_specs=t_blk,
        out_shape=jax.ShapeDtypeStruct((b, h, n_tiles, HEAD_DIM, SB_TILE), F32),
        compiler_params=_params(("parallel", "parallel")),
        name="sb_prompt",
    )(bias, qt, kp, vt)


def _sbs_kernel(pt_ref, qbd_ref, bias_ref, kn_ref, vn_ref, *rest):
    del pt_ref
    k_pages = rest[:PAGES_PER_STEP]
    v_pages = rest[PAGES_PER_STEP:2 * PAGES_PER_STEP]
    o_ref, carry_ref, acc_ref = rest[2 * PAGES_PER_STEP:]
    step = pl.program_id(1)
    n_steps = pl.num_programs(1)
    qbd = qbd_ref[0]
    bias = bias_ref[...]
    tn = (((0,), (0,)), ((), ()))

    def attend(k_f32, v_f32, mask, block):
        n = k_f32.shape[0]
        z = jnp.dot(k_f32.astype(BF16), qbd, preferred_element_type=F32) + bias
        sp = _softplus(z)
        if mask is not None:
            sp = jnp.where(mask, sp, 0.0)
        lk = (-sp).astype(BF16)
        row = lax.broadcasted_iota(jnp.int32, (block, block), 0)
        col = lax.broadcasted_iota(jnp.int32, (block, block), 1)
        tri = jnp.where(col >= row, 1.0, 0.0).astype(BF16)
        n_blocks = n // block
        partial = [jnp.dot(tri, lk[i * block:(i + 1) * block], preferred_element_type=F32)
                   for i in range(n_blocks)]
        carry = carry_ref[...]
        incl = [None] * n_blocks
        for i in reversed(range(n_blocks)):
            incl[i] = partial[i] + carry
            carry = carry + partial[i][0:1, :]
        carry_ref[...] = carry
        a = jnp.exp(z + jnp.concatenate(incl, axis=0))
        if mask is not None:
            a = jnp.where(mask, a, 0.0)
        acc_ref[...] += lax.dot_general(a.astype(BF16), v_f32.astype(BF16), tn,
                                        preferred_element_type=F32)

    @pl.when(step == 0)
    def _():
        carry_ref[...] = jnp.zeros_like(carry_ref)
        acc_ref[...] = jnp.zeros_like(acc_ref)
        n_new = kn_ref.shape[1]
        pad = jnp.zeros((16 - n_new, WIDTH), F32)
        key = lax.broadcasted_iota(jnp.int32, (16, N_HEADS * n_new), 0)
        tok = lax.broadcasted_iota(jnp.int32, (16, N_HEADS * n_new), 1) % n_new
        attend(jnp.concatenate([kn_ref[0], pad], axis=0), jnp.concatenate([vn_ref[0], pad], axis=0),
               key < tok, 16)

    k_all = jnp.concatenate([k_pages[i][0] for i in reversed(range(PAGES_PER_STEP))], axis=0)
    v_all = jnp.concatenate([v_pages[i][0] for i in reversed(range(PAGES_PER_STEP))], axis=0)
    attend(k_all, v_all, None, SAMPLE_SCAN_BLOCK)

    @pl.when(step == n_steps - 1)
    def _():
        n_new = kn_ref.shape[1]
        acc = acc_ref[...]
        o_ref[0] = jnp.concatenate(
            [acc[h * n_new:(h + 1) * n_new, h * HEAD_DIM:(h + 1) * HEAD_DIM] for h in range(N_HEADS)], axis=1)


def _sb_sample(page_table, qbd, bias_lanes, k_new, v_new, cache_k, cache_v):
    n_req, n_pages = page_table.shape
    n_new = k_new.shape[1]
    n_steps = n_pages // PAGES_PER_STEP

    def page_spec(i):
        return pl.BlockSpec((1, PAGE_SIZE, WIDTH),
                            lambda b, s, pt: (pt[b, n_pages - 1 - (PAGES_PER_STEP * s + i)], 0, 0))

    req3 = lambda b, s, pt: (b, 0, 0)
    grid_spec = pltpu.PrefetchScalarGridSpec(
        num_scalar_prefetch=1,
        grid=(n_req, n_steps),
        in_specs=[pl.BlockSpec((1, WIDTH, N_HEADS * n_new), req3),
                  pl.BlockSpec((1, N_HEADS * n_new), lambda b, s, pt: (0, 0)),
                  pl.BlockSpec((1, n_new, WIDTH), req3),
                  pl.BlockSpec((1, n_new, WIDTH), req3)]
                 + [page_spec(i) for i in range(PAGES_PER_STEP)] * 2,
        out_specs=pl.BlockSpec((1, n_new, WIDTH), req3),
        scratch_shapes=[pltpu.VMEM((1, N_HEADS * n_new), F32),
                        pltpu.VMEM((N_HEADS * n_new, WIDTH), F32)],
    )
    return pl.pallas_call(
        _sbs_kernel,
        grid_spec=grid_spec,
        out_shape=jax.ShapeDtypeStruct((n_req, n_new, WIDTH), F32),
        compiler_params=_params(("parallel", "arbitrary")),
        name="sb_sample",
    )(page_table, qbd, bias_lanes, k_new, v_new, *([cache_k] * PAGES_PER_STEP), *([cache_v] * PAGES_PER_STEP))


def _out_mlp_kernel(x_ref, ret_ref, sb_ref, sbg_ref, wo_ref, g_post_ref, g_pre_ref, g_mlp_ref,
                    wu_ref, wd_ref, y_ref, *, sb_transposed):
    def rms(v, g_ref):
        return v * lax.rsqrt(jnp.mean(v * v, axis=-1, keepdims=True) + NORM_EPS) * g_ref[...]

    mix = jnp.dot(ret_ref[...].astype(BF16), wo_ref[:WIDTH, :], preferred_element_type=F32)
    if sb_transposed:
        o = sb_ref[0, :, 0]
        y = o * lax.rsqrt(jnp.mean(o * o, axis=1, keepdims=True) + NORM_EPS) * sbg_ref[...]
        sb = y.reshape(WIDTH, y.shape[-1]).T.astype(BF16)
    else:
        parts = []
        for h in range(N_HEADS):
            sl = slice(h * HEAD_DIM, (h + 1) * HEAD_DIM)
            o = sb_ref[:, sl]
            parts.append(o * lax.rsqrt(jnp.mean(o * o, axis=-1, keepdims=True) + NORM_EPS) * sbg_ref[:, sl])
        sb = jnp.concatenate(parts, axis=1).astype(BF16)
    mix = mix + jnp.dot(sb, wo_ref[WIDTH:, :], preferred_element_type=F32)
    y1 = x_ref[...] + rms(mix, g_post_ref)
    h2 = rms(y1, g_pre_ref).astype(BF16)
    u = jnp.maximum(jnp.dot(h2, wu_ref[...], preferred_element_type=F32), 0.0)
    d = jnp.dot((u * u).astype(BF16), wd_ref[...], preferred_element_type=F32)
    y_ref[...] = y1 + rms(d, g_mlp_ref)


def _out_mlp(x, ret, sb, sb_g, w_out, g_post, g_pre, g_mlp, w_up, w_down, tm, sb_transposed):
    t = x.shape[0]
    tok = lambda i: (i, 0)
    const = lambda i: (0, 0)
    if sb_transposed:
        n_tiles = sb.shape[2]
        sb_spec = pl.BlockSpec((1, N_HEADS, 1, HEAD_DIM, tm), lambda i: (i // n_tiles, 0, i % n_tiles, 0, 0))
        sbg_spec = pl.BlockSpec((N_HEADS, HEAD_DIM, 1), lambda i: (0, 0, 0))
    else:
        sb_spec = pl.BlockSpec((tm, WIDTH), tok)
        sbg_spec = pl.BlockSpec((1, WIDTH), const)
    vec = pl.BlockSpec((1, D_MODEL), const)
    return pl.pallas_call(
        functools.partial(_out_mlp_kernel, sb_transposed=sb_transposed),
        grid=(t // tm,),
        in_specs=[pl.BlockSpec((tm, D_MODEL), tok),
                  pl.BlockSpec((tm, WIDTH), tok),
                  sb_spec, sbg_spec,
                  pl.BlockSpec((2 * WIDTH, D_MODEL), const, pipeline_mode=pl.Buffered(1)),
                  vec, vec, vec,
                  pl.BlockSpec((D_MODEL, D_FF), const, pipeline_mode=pl.Buffered(1)),
                  pl.BlockSpec((D_FF, D_MODEL), const, pipeline_mode=pl.Buffered(1))],
        out_specs=pl.BlockSpec((tm, D_MODEL), tok),
        out_shape=jax.ShapeDtypeStruct((t, D_MODEL), F32),
        compiler_params=_params(("parallel",)),
        name="out_mlp",
    )(x, ret, sb, sb_g, w_out, g_post, g_pre, g_mlp, w_up, w_down)


def _rotary_tables(pos):
    half = HEAD_DIM // 2
    inv = ROPE_BASE ** (-jnp.arange(half, dtype=F32) / half)
    ang = pos[:, None] * inv[None, :]
    cos, sin = jnp.cos(ang), jnp.sin(ang)
    return jnp.tile(jnp.concatenate([cos, cos], axis=1), (1, 2)), jnp.tile(jnp.concatenate([-sin, sin], axis=1), (1, 2))


def kernel(x_prompt, x_sample, cache_sb_k, cache_sb_v, page_table, state_ret, norm_mix_pre, norm_mix_post, w_in, ret_norm_g, sb_bias, sb_norm_g, w_out, norm_mlp_pre, norm_mlp_post, w_up, w_down):
    assert w_in.shape[0] == 1, "single layer"
    batch, seq, _ = x_prompt.shape
    n_req, n_new, _ = x_sample.shape
    n_pages = page_table.shape[1]
    past_len = n_pages * PAGE_SIZE
    n_tiles = seq // SB_TILE

    w_in_b = w_in[0].astype(BF16)
    w_out_b = w_out[0].astype(BF16)
    w_up_b = w_up[0].astype(BF16)
    w_down_b = w_down[0].astype(BF16)
    g_pre = norm_mix_pre
    bias = sb_bias[0].astype(F32)

    tm_p = 512
    cos_p, sin_p = _rotary_tables(jnp.arange(seq, dtype=F32))
    xp = x_prompt.reshape(batch * seq, D_MODEL)
    rq, rk, rv, gate, sq, sk, sv = _proj(xp, g_pre, w_in_b, cos_p, sin_p, tm_p)
    zero_state = jnp.zeros((batch, N_HEADS, HEAD_DIM, HEAD_DIM), F32)
    ret_p, state_p = _retention(rq, rk, rv, gate, zero_state, ret_norm_g, batch, RET_CHUNK)

    qt = sq.reshape(batch, n_tiles, SB_TILE, N_HEADS, HEAD_DIM).transpose(0, 3, 1, 4, 2)
    sk6 = sk.astype(BF16).reshape(batch, n_tiles, SB_GROUPS, SB_GROUP_LEN, N_HEADS, HEAD_DIM)
    kp = sk6.transpose(0, 4, 1, 3, 2, 5).reshape(batch, N_HEADS, n_tiles, SB_TILE, HEAD_DIM)
    sv6 = sv.astype(BF16).reshape(batch, n_tiles, SB_GROUPS, SB_GROUP_LEN, N_HEADS, HEAD_DIM)
    vt = sv6.transpose(0, 4, 1, 5, 3, 2).reshape(batch, N_HEADS, n_tiles, HEAD_DIM, SB_TILE)
    sb_p = _sb_prompt(bias, qt, kp, vt)

    y_p = _out_mlp(xp, ret_p, sb_p, sb_norm_g.reshape(N_HEADS, HEAD_DIM, 1), w_out_b, norm_mix_post,
                   norm_mlp_pre, norm_mlp_post, w_up_b, w_down_b, SB_TILE, True)

    tm_s = 256
    cos_s, sin_s = _rotary_tables(past_len + jnp.arange(n_new, dtype=F32))
    cos_s, sin_s = jnp.tile(cos_s, (tm_s // n_new, 1)), jnp.tile(sin_s, (tm_s // n_new, 1))
    xs = x_sample.reshape(n_req * n_new, D_MODEL)
    rq, rk, rv, gate, sq, sk_s, sv_s = _proj(xs, g_pre, w_in_b, cos_s, sin_s, tm_s)
    ret_s, state_s = _retention(rq, rk, rv, gate, state_ret[0].astype(F32), ret_norm_g, n_req, n_new)

    q4 = sq.reshape(n_req, n_new, N_HEADS, HEAD_DIM).transpose(0, 2, 3, 1)
    eye = jnp.eye(N_HEADS, dtype=BF16)
    qbd = (q4[:, :, :, None, :] * eye[None, :, None, :, None]).reshape(n_req, WIDTH, N_HEADS * n_new)
    bias_lanes = jnp.repeat(bias, n_new)[None, :]
    sb_s = _sb_sample(page_table, qbd, bias_lanes, sk_s.reshape(n_req, n_new, WIDTH),
                      sv_s.reshape(n_req, n_new, WIDTH),
                      cache_sb_k[0].reshape(-1, PAGE_SIZE, WIDTH), cache_sb_v[0].reshape(-1, PAGE_SIZE, WIDTH))
    y_s = _out_mlp(xs, ret_s, sb_s.reshape(n_req * n_new, WIDTH), sb_norm_g, w_out_b, norm_mix_post,
                   norm_mlp_pre, norm_mlp_post, w_up_b, w_down_b, tm_s, False)

    dt = x_prompt.dtype
    return (y_p.reshape(batch, seq, D_MODEL).astype(dt),
            y_s.reshape(n_req, n_new, D_MODEL).astype(dt),
            sk.reshape(1, batch, seq, N_HEADS, HEAD_DIM).astype(dt),
            sv.reshape(1, batch, seq, N_HEADS, HEAD_DIM).astype(dt),
            state_p[None].astype(dt),
            sk_s.reshape(1, n_req, n_new, N_HEADS, HEAD_DIM).astype(cache_sb_k.dtype),
            sv_s.reshape(1, n_req, n_new, N_HEADS, HEAD_DIM).astype(cache_sb_v.dtype),
            state_s[None].astype(state_ret.dtype))
```

```python
import functools

import numpy as np
import jax
import jax.numpy as jnp
from jax import lax
from jax.experimental import pallas as pl
from jax.experimental.pallas import tpu as pltpu

F32 = jnp.float32
BF16 = jnp.bfloat16

D_MODEL = 1024
HEAD_DIM = 64
N_HEADS = 8
WIDTH = N_HEADS * HEAD_DIM
N_SEG = 7
D_FF = 4 * D_MODEL
RET_CHUNK = 128
PAGE_SIZE = 128
ROPE_BASE = 10000.0
NORM_EPS = 1e-6
QK_SCALE = HEAD_DIM ** -0.5
LOG2E = float(np.log2(np.e))

SB_TILE = 256
SB_GROUPS = 8
SB_GROUP_LEN = SB_TILE // SB_GROUPS
SB_HEADS = 4
PAGES_PER_STEP = 16
SAMPLE_GROUP = 8
SAMPLE_TILE = 2 * PAGE_SIZE

_LOG_GAMMA = [float(np.log1p(-np.exp2(np.float32(-5.0 - h)), dtype=np.float32)) for h in range(N_HEADS)]

_VMEM_LIMIT = 56 * 1024 * 1024


def _params(semantics, flags=None):
    return pltpu.CompilerParams(dimension_semantics=semantics, vmem_limit_bytes=_VMEM_LIMIT, flags=flags)


def _softplus(z):
    return jnp.maximum(z, 0.0) + jnp.log(1.0 + jnp.exp2(jnp.abs(z) * -LOG2E))


def _proj_kernel(x_ref, g_ref, w_ref, cos_ref, sin_ref,
                 rq_ref, rk_ref, rv_ref, gate_ref, sq_ref, sk_ref, sv_ref, *, sb_transposed):
    x = x_ref[...]
    h = x * lax.rsqrt(jnp.mean(x * x, axis=-1, keepdims=True) + NORM_EPS) * g_ref[...]
    hb = h.astype(BF16)

    def seg(i):
        return jnp.dot(hb, w_ref[:, i * WIDTH:(i + 1) * WIDTH], preferred_element_type=F32)

    cos = jnp.concatenate([cos_ref[...]] * 4, axis=1)
    sin = jnp.concatenate([sin_ref[...]] * 4, axis=1)
    lane = lax.broadcasted_iota(jnp.int32, cos.shape, 1)
    first_half = (lane % HEAD_DIM) < (HEAD_DIM // 2)

    def rotary(p):
        swapped = jnp.where(first_half, pltpu.roll(p, WIDTH - HEAD_DIM // 2, 1), pltpu.roll(p, HEAD_DIM // 2, 1))
        return p * cos + swapped * sin

    rq_ref[...] = rotary(seg(0))
    rk_ref[...] = rotary(seg(1)) * QK_SCALE
    rv_ref[...] = seg(2)
    gate_ref[...] = seg(3)
    if sb_transposed:
        sq_ref[0] = (seg(4) * QK_SCALE).T.astype(BF16)
        sk_ref[0] = seg(5).T
        sv_ref[0] = seg(6).T
    else:
        sq_ref[...] = (seg(4) * QK_SCALE).astype(BF16)
        sk_ref[...] = seg(5)
        sv_ref[...] = seg(6)


def _proj(x, g, w_bf16, cos_tab, sin_tab, tm, n_seq, sb_transposed):
    t = x.shape[0]
    n_tab = cos_tab.shape[0] // tm
    tok = lambda i: (i, 0)
    tab = lambda i: (i % n_tab, 0)
    const = lambda i: (0, 0)
    out_f32 = jax.ShapeDtypeStruct((t, WIDTH), F32)
    blk = pl.BlockSpec((tm, WIDTH), tok)
    if sb_transposed:
        per_seq = t // (n_seq * tm)
        sb_blk = pl.BlockSpec((1, WIDTH, tm), lambda i: (i // per_seq, 0, i % per_seq))
        sb_shape = (n_seq, WIDTH, t // n_seq)
    else:
        sb_blk = blk
        sb_shape = (t, WIDTH)
    return pl.pallas_call(
        functools.partial(_proj_kernel, sb_transposed=sb_transposed),
        grid=(t // tm,),
        in_specs=[pl.BlockSpec((tm, D_MODEL), tok),
                  pl.BlockSpec((1, D_MODEL), const),
                  pl.BlockSpec((D_MODEL, N_SEG * WIDTH), const),
                  pl.BlockSpec((tm, 128), tab),
                  pl.BlockSpec((tm, 128), tab)],
        out_specs=[blk] * 4 + [sb_blk] * 3,
        out_shape=[out_f32] * 4 + [jax.ShapeDtypeStruct(sb_shape, BF16),
                                   jax.ShapeDtypeStruct(sb_shape, F32), jax.ShapeDtypeStruct(sb_shape, F32)],
        compiler_params=_params(("parallel",)),
        name="proj",
    )(x, g, w_bf16, cos_tab, sin_tab)


def _ret_kernel(q_ref, k_ref, v_ref, gate_ref, s0_ref, g_ref, o_ref, s_ref, *, chunk):
    @pl.when(pl.program_id(1) == 0)
    def _():
        s_ref[...] = s0_ref[...]

    row = lax.broadcasted_iota(jnp.int32, (chunk, chunk), 0)
    col = lax.broadcasted_iota(jnp.int32, (chunk, chunk), 1)
    diff = (row - col).astype(F32)
    idx = lax.broadcasted_iota(jnp.int32, (chunk, 1), 0).astype(F32)
    nt = (((1,), (1,)), ((), ()))
    tn = (((0,), (0,)), ((), ()))
    for h in range(N_HEADS):
        lg = _LOG_GAMMA[h]
        sl = slice(h * HEAD_DIM, (h + 1) * HEAD_DIM)
        qh = q_ref[:, sl].astype(BF16)
        kf = k_ref[:, sl]
        vh = v_ref[:, sl].astype(BF16)
        decay = jnp.where(diff >= 0, jnp.exp(lg * jnp.maximum(diff, 0.0)), 0.0)
        scores = lax.dot_general(qh, kf.astype(BF16), nt, preferred_element_type=F32) * decay
        intra = jnp.dot(scores.astype(BF16), vh, preferred_element_type=F32)
        state = s_ref[0, h]
        cross = jnp.dot(qh, state.astype(BF16), preferred_element_type=F32) * jnp.exp(lg * (idx + 1.0))
        o = intra + cross
        kd = (kf * jnp.exp(lg * (chunk - 1.0 - idx))).astype(BF16)
        s_ref[0, h] = (jnp.exp(jnp.full((1, 1), lg * chunk, F32)) * state
                       + lax.dot_general(kd, vh, tn, preferred_element_type=F32))
        o = o - jnp.mean(o, axis=-1, keepdims=True)
        y = o * lax.rsqrt(jnp.mean(o * o, axis=-1, keepdims=True) + NORM_EPS)
        gate = gate_ref[:, sl]
        o_ref[:, sl] = y * g_ref[:, sl] * (gate / (1.0 + jnp.exp(-gate)))


def _retention(rq, rk, rv, gate, state0, g, n_seq, chunk):
    t = rq.shape[0]
    n_chunks = t // (n_seq * chunk)
    tok = lambda b, c: (b * n_chunks + c, 0)
    st = lambda b, c: (b, 0, 0, 0)
    blk = pl.BlockSpec((chunk, WIDTH), tok)
    st_blk = pl.BlockSpec((1, N_HEADS, HEAD_DIM, HEAD_DIM), st)
    return pl.pallas_call(
        functools.partial(_ret_kernel, chunk=chunk),
        grid=(n_seq, n_chunks),
        in_specs=[blk, blk, blk, blk, st_blk, pl.BlockSpec((1, WIDTH), lambda b, c: (0, 0))],
        out_specs=[blk, st_blk],
        out_shape=[jax.ShapeDtypeStruct((t, WIDTH), F32),
                   jax.ShapeDtypeStruct((n_seq, N_HEADS, HEAD_DIM, HEAD_DIM), F32)],
        compiler_params=_params(("parallel", "arbitrary")),
        name="retention",
    )(rq, rk, rv, gate, state0, g)


def _sbp_kernel(bias_ref, qt_ref, kt_ref, vt_ref, o_ref, k_s, v_s, qm_s, s_s, w_s, a_s, acc_s):
    n_tiles = k_s.shape[0]
    rows = SB_HEADS * HEAD_DIM
    heads = range(SB_HEADS)
    group = pl.program_id(1)
    bias = [bias_ref[group * SB_HEADS + hh] for hh in heads]
    grp = lax.broadcasted_iota(jnp.int32, (SB_GROUPS, SB_TILE), 0)
    qry = lax.broadcasted_iota(jnp.int32, (SB_GROUPS, SB_TILE), 1)
    diag_bound = qry - SB_GROUP_LEN * grp

    pi = lax.broadcasted_iota(jnp.int32, (SB_TILE, SB_TILE), 0)
    pk = lax.broadcasted_iota(jnp.int32, (SB_TILE, SB_TILE), 1)
    perm = jnp.where(pk == (pi % SB_GROUPS) * SB_GROUP_LEN + pi // SB_GROUPS, 1.0, 0.0).astype(BF16)
    nt = (((1,), (1,)), ((), ()))

    def permute(c, _):
        sl = pl.ds(pl.multiple_of(c * SB_TILE, SB_TILE), SB_TILE)
        kt = kt_ref[0, :, sl].astype(BF16)
        vt = vt_ref[0, :, sl].astype(BF16)
        k_s[c] = lax.dot_general(perm, kt, nt, preferred_element_type=F32).astype(BF16)
        v_s[c] = lax.dot_general(vt, perm, nt, preferred_element_type=F32).astype(BF16)
        return 0

    lax.fori_loop(0, n_tiles, permute, 0)

    def scores(c, hh):
        s_s[hh] = jnp.dot(k_s[c], qm_s[hh], preferred_element_type=F32)

    def weighted_values(c, hh):
        return jnp.dot(v_s[c, hh * HEAD_DIM:(hh + 1) * HEAD_DIM, :], a_s[hh], preferred_element_type=F32)

    def pass1(hh, carry, masked):
        run = jnp.zeros((SB_GROUPS, SB_TILE), F32)
        for j in reversed(range(SB_GROUP_LEN)):
            sl = slice(SB_GROUPS * j, SB_GROUPS * (j + 1))
            z = s_s[hh, sl, :] + bias[hh]
            sp = _softplus(z)
            if masked:
                sp = jnp.where(diag_bound > j, sp, 0.0)
            run = run - sp
            w_s[hh, sl, :] = z + run
        scan = run
        for sh in (1, 2, 4):
            scan = scan + jnp.where(grp + sh < SB_GROUPS, pltpu.roll(scan, SB_GROUPS - sh, 0), 0.0)
        offset = scan - run + carry
        return offset, jnp.broadcast_to(scan[0:1, :], carry.shape) + carry

    def pass2(hh, offset, masked):
        for m in range(SB_GROUP_LEN // 2):
            pair = []
            for j in (2 * m, 2 * m + 1):
                a = jnp.exp(w_s[hh, SB_GROUPS * j:SB_GROUPS * (j + 1), :] + offset)
                if masked:
                    a = jnp.where(diag_bound > j, a, 0.0)
                pair.append(a)
            a_s[hh, 2 * SB_GROUPS * m:2 * SB_GROUPS * (m + 1), :] = jnp.concatenate(pair, axis=0).astype(BF16)

    head_of_row = lax.broadcasted_iota(jnp.int32, (rows, SB_TILE), 0) // HEAD_DIM

    def qblock(qb, _):
        qt = qt_ref[0, :, pl.ds(pl.multiple_of(qb * SB_TILE, SB_TILE), SB_TILE)]
        for hh in heads:
            qm_s[hh] = jnp.where(head_of_row == hh, qt, jnp.zeros_like(qt))
        acc_s[...] = jnp.zeros_like(acc_s)
        for hh in heads:
            scores(qb, hh)
        zero_c = jnp.zeros((SB_GROUPS, SB_TILE), F32)
        first = jnp.maximum(qb - 1, 0)
        state = []
        for hh in heads:
            state.append(pass1(hh, zero_c, True))
            scores(first, hh)
        for hh in heads:
            pass2(hh, state[hh][0], True)

        def body(i, carries):
            c = qb - 1 - i
            pv = [weighted_values(c + 1, hh) for hh in heads]
            nxt = jnp.maximum(c - 1, 0)
            state = []
            for hh in heads:
                state.append(pass1(hh, carries[hh], False))
                scores(nxt, hh)
            for hh in heads:
                pass2(hh, state[hh][0], False)
            for hh in heads:
                acc_s[hh] += pv[hh]
            return tuple(st[1] for st in state)

        lax.fori_loop(0, qb, body, tuple(st[1] for st in state))
        for hh in heads:
            o_ref[0, hh, qb] = acc_s[hh] + weighted_values(0, hh)
        return 0

    lax.fori_loop(0, n_tiles, qblock, 0)


def _sb_prompt(bias, qt, kt, vt):
    b, _, seq = qt.shape
    n_tiles = seq // SB_TILE
    rows = SB_HEADS * HEAD_DIM
    in_blk = pl.BlockSpec((1, rows, seq), lambda i, g: (i, g, 0))
    return pl.pallas_call(
        _sbp_kernel,
        grid=(b, N_HEADS // SB_HEADS),
        in_specs=[pl.BlockSpec(memory_space=pltpu.SMEM), in_blk, in_blk, in_blk],
        out_specs=pl.BlockSpec((1, SB_HEADS, n_tiles, HEAD_DIM, SB_TILE), lambda i, g: (i, g, 0, 0, 0)),
        out_shape=jax.ShapeDtypeStruct((b, N_HEADS, n_tiles, HEAD_DIM, SB_TILE), F32),
        scratch_shapes=[pltpu.VMEM((n_tiles, SB_TILE, rows), BF16),
                        pltpu.VMEM((n_tiles, rows, SB_TILE), BF16),
                        pltpu.VMEM((SB_HEADS, rows, SB_TILE), BF16),
                        pltpu.VMEM((SB_HEADS, SB_TILE, SB_TILE), F32),
                        pltpu.VMEM((SB_HEADS, SB_TILE, SB_TILE), F32),
                        pltpu.VMEM((SB_HEADS, SB_TILE, SB_TILE), BF16),
                        pltpu.VMEM((SB_HEADS, HEAD_DIM, SB_TILE), F32)],
        compiler_params=_params(("parallel", "parallel")),
        name="sb_prompt",
    )(bias, qt, kt, vt)


def _sbs_kernel(pt_ref, qbd_ref, bias_ref, kn_ref, vn_ref, *rest):
    del pt_ref
    k_pages = rest[:PAGES_PER_STEP]
    v_pages = rest[PAGES_PER_STEP:2 * PAGES_PER_STEP]
    o_ref, carry_ref, acc_ref = rest[2 * PAGES_PER_STEP:]
    step = pl.program_id(1)
    n_steps = pl.num_programs(1)
    n_new = kn_ref.shape[1]
    n_q = N_HEADS * n_new
    w = SAMPLE_TILE
    qbd = qbd_ref[0]
    bias = bias_ref[...]
    nt = (((1,), (1,)), ((), ()))
    row = lax.broadcasted_iota(jnp.int32, (w, 2 * w), 0)
    col = lax.broadcasted_iota(jnp.int32, (w, 2 * w), 1)
    later = jnp.where((row > col) | (col >= w), 1.0, 0.0).astype(BF16)

    def tile_sums(z, mask):
        sp = _softplus(z)
        if mask is not None:
            sp = jnp.where(mask, sp, 0.0)
        lk = (-sp).astype(BF16)
        stacked = jnp.concatenate([lk[:, t * w:(t + 1) * w] for t in range(z.shape[1] // w)], axis=0)
        return sp, jnp.dot(stacked, later, preferred_element_type=F32)

    def weights(z, sp, sums, mask, carry):
        tiles = [None] * (z.shape[1] // w)
        for t in reversed(range(len(tiles))):
            s = sums[t * n_q:(t + 1) * n_q]
            tiles[t] = jnp.exp(z[:, t * w:(t + 1) * w] - sp[:, t * w:(t + 1) * w] + s[:, :w] + carry)
            carry = carry + s[:, w:]
        a = jnp.concatenate(tiles, axis=1)
        if mask is not None:
            a = jnp.where(mask, a, 0.0)
        pad_q = jnp.zeros((128 - n_q, a.shape[1]), F32)
        return jnp.concatenate([a, pad_q], axis=0).T.astype(BF16), carry

    @pl.when(step == 0)
    def _():
        pad = jnp.zeros((w - n_new, WIDTH), F32)
        kn = jnp.concatenate([kn_ref[0], pad], axis=0).astype(BF16)
        vnt = jnp.concatenate([vn_ref[0], pad], axis=0).T.astype(BF16)
        z = lax.dot_general(qbd, kn, nt, preferred_element_type=F32) + bias
        key = lax.broadcasted_iota(jnp.int32, (n_q, w), 1)
        tok = lax.broadcasted_iota(jnp.int32, (n_q, w), 0) % n_new
        mask = key < tok
        sp, sums = tile_sums(z, mask)
        at, carry = weights(z, sp, sums, mask, jnp.zeros((n_q, w), F32))
        carry_ref[...] = carry
        acc_ref[...] = jnp.dot(vnt, at, preferred_element_type=F32)

    def gather(pages, g):
        slots = reversed(range(g * SAMPLE_GROUP, (g + 1) * SAMPLE_GROUP))
        return jnp.concatenate([pages[i][0].reshape(WIDTH, PAGE_SIZE) for i in slots], axis=1).astype(BF16)

    groups = range(PAGES_PER_STEP // SAMPLE_GROUP)
    zs = [jnp.dot(qbd, gather(k_pages, g), preferred_element_type=F32) + bias for g in groups]
    sums = [tile_sums(z, None) for z in zs]
    carry = carry_ref[...]
    ats = []
    for g in groups:
        at, carry = weights(zs[g], sums[g][0], sums[g][1], None, carry)
        ats.append(at)
    carry_ref[...] = carry
    pv = [jnp.dot(gather(v_pages, g), ats[g], preferred_element_type=F32) for g in groups]
    acc_ref[...] += functools.reduce(lambda x, y: x + y, pv)

    @pl.when(step == n_steps - 1)
    def _():
        acct = acc_ref[...].T
        o_ref[0] = jnp.concatenate(
            [acct[h * n_new:(h + 1) * n_new, h * HEAD_DIM:(h + 1) * HEAD_DIM] for h in range(N_HEADS)], axis=1)


def _sb_sample(page_table, qbd, bias_col, k_new, v_new, cache_k, cache_v):
    n_req, n_pages = page_table.shape
    n_new = k_new.shape[1]
    n_q = N_HEADS * n_new
    n_steps = n_pages // PAGES_PER_STEP

    def page_spec(i):
        return pl.BlockSpec((1, N_HEADS, HEAD_DIM, PAGE_SIZE),
                            lambda b, s, pt: (pt[b, n_pages - 1 - (PAGES_PER_STEP * s + i)], 0, 0, 0))

    req3 = lambda b, s, pt: (b, 0, 0)
    grid_spec = pltpu.PrefetchScalarGridSpec(
        num_scalar_prefetch=1,
        grid=(n_req, n_steps),
        in_specs=[pl.BlockSpec((1, n_q, WIDTH), req3),
                  pl.BlockSpec((n_q, 1), lambda b, s, pt: (0, 0)),
                  pl.BlockSpec((1, n_new, WIDTH), req3),
                  pl.BlockSpec((1, n_new, WIDTH), req3)]
                 + [page_spec(i) for i in range(PAGES_PER_STEP)] * 2,
        out_specs=pl.BlockSpec((1, n_new, WIDTH), req3),
        scratch_shapes=[pltpu.VMEM((n_q, SAMPLE_TILE), F32),
                        pltpu.VMEM((WIDTH, 128), F32)],
    )
    return pl.pallas_call(
        _sbs_kernel,
        grid_spec=grid_spec,
        out_shape=jax.ShapeDtypeStruct((n_req, n_new, WIDTH), F32),
        compiler_params=_params(("parallel", "arbitrary")),
        name="sb_sample",
    )(page_table, qbd, bias_col, k_new, v_new, *([cache_k] * PAGES_PER_STEP), *([cache_v] * PAGES_PER_STEP))


def _out_mlp_kernel(x_ref, ret_ref, sb_ref, sbg_ref, wo_ref, g_post_ref, g_pre_ref, g_mlp_ref,
                    wu_ref, wd_ref, y_ref, *, sb_transposed):
    def rms(v, g_ref):
        return v * lax.rsqrt(jnp.mean(v * v, axis=-1, keepdims=True) + NORM_EPS) * g_ref[...]

    mix = jnp.dot(ret_ref[...].astype(BF16), wo_ref[:WIDTH, :], preferred_element_type=F32)
    if sb_transposed:
        o = sb_ref[0, :, 0]
        y = o * lax.rsqrt(jnp.mean(o * o, axis=1, keepdims=True) + NORM_EPS) * sbg_ref[...]
        sb = y.reshape(WIDTH, y.shape[-1]).T.astype(BF16)
    else:
        parts = []
        for h in range(N_HEADS):
            sl = slice(h * HEAD_DIM, (h + 1) * HEAD_DIM)
            o = sb_ref[:, sl]
            parts.append(o * lax.rsqrt(jnp.mean(o * o, axis=-1, keepdims=True) + NORM_EPS) * sbg_ref[:, sl])
        sb = jnp.concatenate(parts, axis=1).astype(BF16)
    mix = mix + jnp.dot(sb, wo_ref[WIDTH:, :], preferred_element_type=F32)
    y1 = x_ref[...] + rms(mix, g_post_ref)
    h2 = rms(y1, g_pre_ref).astype(BF16)
    u = jnp.maximum(jnp.dot(h2, wu_ref[...], preferred_element_type=F32), 0.0)
    d = jnp.dot((u * u).astype(BF16), wd_ref[...], preferred_element_type=F32)
    y_ref[...] = y1 + rms(d, g_mlp_ref)


def _out_mlp(x, ret, sb, sb_g, w_out, g_post, g_pre, g_mlp, w_up, w_down, tm, sb_transposed):
    t = x.shape[0]
    tok = lambda i: (i, 0)
    const = lambda i: (0, 0)
    if sb_transposed:
        n_tiles = sb.shape[2]
        sb_spec = pl.BlockSpec((1, N_HEADS, 1, HEAD_DIM, tm), lambda i: (i // n_tiles, 0, i % n_tiles, 0, 0))
        sbg_spec = pl.BlockSpec((N_HEADS, HEAD_DIM, 1), lambda i: (0, 0, 0))
    else:
        sb_spec = pl.BlockSpec((tm, WIDTH), tok)
        sbg_spec = pl.BlockSpec((1, WIDTH), const)
    vec = pl.BlockSpec((1, D_MODEL), const)
    return pl.pallas_call(
        functools.partial(_out_mlp_kernel, sb_transposed=sb_transposed),
        grid=(t // tm,),
        in_specs=[pl.BlockSpec((tm, D_MODEL), tok),
                  pl.BlockSpec((tm, WIDTH), tok),
                  sb_spec, sbg_spec,
                  pl.BlockSpec((2 * WIDTH, D_MODEL), const, pipeline_mode=pl.Buffered(1)),
                  vec, vec, vec,
                  pl.BlockSpec((D_MODEL, D_FF), const, pipeline_mode=pl.Buffered(1)),
                  pl.BlockSpec((D_FF, D_MODEL), const, pipeline_mode=pl.Buffered(1))],
        out_specs=pl.BlockSpec((tm, D_MODEL), tok),
        out_shape=jax.ShapeDtypeStruct((t, D_MODEL), F32),
        compiler_params=_params(("parallel",)),
        name="out_mlp",
    )(x, ret, sb, sb_g, w_out, g_post, g_pre, g_mlp, w_up, w_down)


def _rotary_tables(pos):
    half = HEAD_DIM // 2
    inv = ROPE_BASE ** (-jnp.arange(half, dtype=F32) / half)
    ang = pos[:, None] * inv[None, :]
    cos, sin = jnp.cos(ang), jnp.sin(ang)
    return jnp.tile(jnp.concatenate([cos, cos], axis=1), (1, 2)), jnp.tile(jnp.concatenate([-sin, sin], axis=1), (1, 2))


def kernel(x_prompt, x_sample, cache_sb_k, cache_sb_v, page_table, state_ret, norm_mix_pre, norm_mix_post, w_in, ret_norm_g, sb_bias, sb_norm_g, w_out, norm_mlp_pre, norm_mlp_post, w_up, w_down):
    assert w_in.shape[0] == 1, "single layer"
    batch, seq, _ = x_prompt.shape
    n_req, n_new, _ = x_sample.shape
    n_pages = page_table.shape[1]
    past_len = n_pages * PAGE_SIZE

    w_in_b = w_in[0].astype(BF16)
    w_out_b = w_out[0].astype(BF16)
    w_up_b = w_up[0].astype(BF16)
    w_down_b = w_down[0].astype(BF16)
    bias = sb_bias[0].astype(F32)
    dt = x_prompt.dtype

    tm_p = 512
    cos_p, sin_p = _rotary_tables(jnp.arange(seq, dtype=F32))
    xp = x_prompt.reshape(batch * seq, D_MODEL)
    rq, rk, rv, gate, sqt, skt, svt = _proj(xp, norm_mix_pre, w_in_b, cos_p, sin_p, tm_p, batch, True)
    zero_state = jnp.zeros((batch, N_HEADS, HEAD_DIM, HEAD_DIM), F32)
    ret_p, state_p = _retention(rq, rk, rv, gate, zero_state, ret_norm_g, batch, RET_CHUNK)
    sb_p = _sb_prompt(bias, sqt, skt, svt)
    y_p = _out_mlp(xp, ret_p, sb_p, sb_norm_g.reshape(N_HEADS, HEAD_DIM, 1), w_out_b, norm_mix_post,
                   norm_mlp_pre, norm_mlp_post, w_up_b, w_down_b, SB_TILE, True)
    k_prompt = skt.reshape(1, batch, N_HEADS, HEAD_DIM, seq).transpose(0, 1, 4, 2, 3)
    v_prompt = svt.reshape(1, batch, N_HEADS, HEAD_DIM, seq).transpose(0, 1, 4, 2, 3)

    tm_s = 256
    cos_s, sin_s = _rotary_tables(past_len + jnp.arange(n_new, dtype=F32))
    cos_s, sin_s = jnp.tile(cos_s, (tm_s // n_new, 1)), jnp.tile(sin_s, (tm_s // n_new, 1))
    xs = x_sample.reshape(n_req * n_new, D_MODEL)
    rq, rk, rv, gate, sq, sk_s, sv_s = _proj(xs, norm_mix_pre, w_in_b, cos_s, sin_s, tm_s, n_req, False)
    ret_s, state_s = _retention(rq, rk, rv, gate, state_ret[0].astype(F32), ret_norm_g, n_req, n_new)

    q4 = sq.reshape(n_req, n_new, N_HEADS, HEAD_DIM).transpose(0, 2, 1, 3)
    eye = jnp.eye(N_HEADS, dtype=BF16)
    qbd = (q4[:, :, :, None, :] * eye[None, :, None, :, None]).reshape(n_req, N_HEADS * n_new, WIDTH)
    bias_col = jnp.repeat(bias, n_new)[:, None]
    cache_k = cache_sb_k[0].transpose(0, 2, 3, 1)
    cache_v = cache_sb_v[0].transpose(0, 2, 3, 1)
    sb_s = _sb_sample(page_table, qbd, bias_col, sk_s.reshape(n_req, n_new, WIDTH),
                      sv_s.reshape(n_req, n_new, WIDTH), cache_k, cache_v)
    y_s = _out_mlp(xs, ret_s, sb_s.reshape(n_req * n_new, WIDTH), sb_norm_g, w_out_b, norm_mix_post,
                   norm_mlp_pre, norm_mlp_post, w_up_b, w_down_b, tm_s, False)

    return (y_p.reshape(batch, seq, D_MODEL).astype(dt),
            y_s.reshape(n_req, n_new, D_MODEL).astype(dt),
            k_prompt.astype(dt),
            v_prompt.astype(dt),
            state_p[None].astype(dt),
            sk_s.reshape(1, n_req, n_new, N_HEADS, HEAD_DIM).astype(cache_sb_k.dtype),
            sv_s.reshape(1, n_req, n_new, N_HEADS, HEAD_DIM).astype(cache_sb_v.dtype),
            state_s[None].astype(state_ret.dtype))
```

```python
import functools

import numpy as np
import jax
import jax.numpy as jnp
from jax import lax
from jax.experimental import pallas as pl
from jax.experimental.pallas import tpu as pltpu

F32 = jnp.float32
BF16 = jnp.bfloat16

D_MODEL = 1024
HEAD_DIM = 64
N_HEADS = 8
WIDTH = N_HEADS * HEAD_DIM
N_SEG = 7
D_FF = 4 * D_MODEL
RET_CHUNK = 128
PAGE_SIZE = 128
ROPE_BASE = 10000.0
NORM_EPS = 1e-6
QK_SCALE = HEAD_DIM ** -0.5
LOG2E = float(np.log2(np.e))

SB_TILE = 256
SB_GROUPS = 8
SB_GROUP_LEN = SB_TILE // SB_GROUPS
SB_HEADS = 4
SAMPLE_UNIT = 8
SAMPLE_TILE = 2 * PAGE_SIZE

_LOG_GAMMA = [float(np.log1p(-np.exp2(np.float32(-5.0 - h)), dtype=np.float32)) for h in range(N_HEADS)]

_VMEM_LIMIT = 56 * 1024 * 1024
_VMEM_LIMIT_SB = 60 * 1024 * 1024


def _params(semantics, vmem_limit=_VMEM_LIMIT):
    return pltpu.CompilerParams(dimension_semantics=semantics, vmem_limit_bytes=vmem_limit)


def _softplus(z):
    return jnp.maximum(z, 0.0) + jnp.log(1.0 + jnp.exp2(jnp.abs(z) * -LOG2E))


def _proj_kernel(x_ref, g_ref, w_ref, cos_ref, sin_ref,
                 rq_ref, rk_ref, rv_ref, gate_ref, sq_ref, sk_ref, sv_ref, *, sb_transposed):
    x = x_ref[...]
    h = x * lax.rsqrt(jnp.mean(x * x, axis=-1, keepdims=True) + NORM_EPS) * g_ref[...]
    hb = h.astype(BF16)

    def seg(i):
        return jnp.dot(hb, w_ref[:, i * WIDTH:(i + 1) * WIDTH], preferred_element_type=F32)

    cos = jnp.concatenate([cos_ref[...]] * 4, axis=1)
    sin = jnp.concatenate([sin_ref[...]] * 4, axis=1)
    lane = lax.broadcasted_iota(jnp.int32, cos.shape, 1)
    first_half = (lane % HEAD_DIM) < (HEAD_DIM // 2)

    def rotary(p):
        swapped = jnp.where(first_half, pltpu.roll(p, WIDTH - HEAD_DIM // 2, 1), pltpu.roll(p, HEAD_DIM // 2, 1))
        return p * cos + swapped * sin

    rq_ref[...] = rotary(seg(0))
    rk_ref[...] = rotary(seg(1)) * QK_SCALE
    rv_ref[...] = seg(2)
    gate_ref[...] = seg(3)
    if sb_transposed:
        sq_ref[0] = (seg(4) * QK_SCALE).T.astype(BF16)
        sk_ref[0] = seg(5).T
        sv_ref[0] = seg(6).T
    else:
        sq_ref[...] = (seg(4) * QK_SCALE).astype(BF16)
        sk_ref[...] = seg(5)
        sv_ref[...] = seg(6)


def _proj(x, g, w_bf16, cos_tab, sin_tab, tm, n_seq, sb_transposed):
    t = x.shape[0]
    n_tab = cos_tab.shape[0] // tm
    tok = lambda i: (i, 0)
    tab = lambda i: (i % n_tab, 0)
    const = lambda i: (0, 0)
    out_f32 = jax.ShapeDtypeStruct((t, WIDTH), F32)
    blk = pl.BlockSpec((tm, WIDTH), tok)
    if sb_transposed:
        per_seq = t // (n_seq * tm)
        sb_blk = pl.BlockSpec((1, WIDTH, tm), lambda i: (i // per_seq, 0, i % per_seq))
        sb_shape = (n_seq, WIDTH, t // n_seq)
    else:
        sb_blk = blk
        sb_shape = (t, WIDTH)
    return pl.pallas_call(
        functools.partial(_proj_kernel, sb_transposed=sb_transposed),
        grid=(t // tm,),
        in_specs=[pl.BlockSpec((tm, D_MODEL), tok),
                  pl.BlockSpec((1, D_MODEL), const),
                  pl.BlockSpec((D_MODEL, N_SEG * WIDTH), const),
                  pl.BlockSpec((tm, 128), tab),
                  pl.BlockSpec((tm, 128), tab)],
        out_specs=[blk] * 4 + [sb_blk] * 3,
        out_shape=[out_f32] * 4 + [jax.ShapeDtypeStruct(sb_shape, BF16),
                                   jax.ShapeDtypeStruct(sb_shape, F32), jax.ShapeDtypeStruct(sb_shape, F32)],
        compiler_params=_params(("parallel",)),
        name="proj",
    )(x, g, w_bf16, cos_tab, sin_tab)


def _ret_kernel(q_ref, k_ref, v_ref, gate_ref, s0_ref, g_ref, o_ref, s_ref, decay_s, *, chunk):
    heads = range(N_HEADS)
    sl = [slice(h * HEAD_DIM, (h + 1) * HEAD_DIM) for h in heads]

    @pl.when(pl.program_id(1) == 0)
    def _():
        s_ref[...] = s0_ref[...]
        row = lax.broadcasted_iota(jnp.int32, (chunk, chunk), 0)
        col = lax.broadcasted_iota(jnp.int32, (chunk, chunk), 1)
        diff = (row - col).astype(F32)
        for h in heads:
            decay_s[h] = jnp.where(diff >= 0, jnp.exp(_LOG_GAMMA[h] * jnp.maximum(diff, 0.0)), 0.0)

    idx = lax.broadcasted_iota(jnp.int32, (chunk, 1), 0).astype(F32)
    nt = (((1,), (1,)), ((), ()))
    tn = (((0,), (0,)), ((), ()))
    qh = [q_ref[:, sl[h]].astype(BF16) for h in heads]
    kf = [k_ref[:, sl[h]] for h in heads]
    vh = [v_ref[:, sl[h]].astype(BF16) for h in heads]
    state = [s_ref[0, h] for h in heads]
    scores = [lax.dot_general(qh[h], kf[h].astype(BF16), nt, preferred_element_type=F32) for h in heads]
    cross = [jnp.dot(qh[h], state[h].astype(BF16), preferred_element_type=F32) for h in heads]
    kd = [(kf[h] * jnp.exp(_LOG_GAMMA[h] * (chunk - 1.0 - idx))).astype(BF16) for h in heads]
    kv = [lax.dot_general(kd[h], vh[h], tn, preferred_element_type=F32) for h in heads]
    intra = [jnp.dot((scores[h] * decay_s[h]).astype(BF16), vh[h], preferred_element_type=F32) for h in heads]
    for h in heads:
        lg = _LOG_GAMMA[h]
        s_ref[0, h] = jnp.exp(jnp.full((1, 1), lg * chunk, F32)) * state[h] + kv[h]
        o = intra[h] + cross[h] * jnp.exp(lg * (idx + 1.0))
        o = o - jnp.mean(o, axis=-1, keepdims=True)
        y = o * lax.rsqrt(jnp.mean(o * o, axis=-1, keepdims=True) + NORM_EPS)
        gate = gate_ref[:, sl[h]]
        o_ref[:, sl[h]] = y * g_ref[:, sl[h]] * (gate / (1.0 + jnp.exp(-gate)))


def _retention(rq, rk, rv, gate, state0, g, n_seq, chunk):
    t = rq.shape[0]
    n_chunks = t // (n_seq * chunk)
    tok = lambda b, c: (b * n_chunks + c, 0)
    st = lambda b, c: (b, 0, 0, 0)
    blk = pl.BlockSpec((chunk, WIDTH), tok)
    st_blk = pl.BlockSpec((1, N_HEADS, HEAD_DIM, HEAD_DIM), st)
    return pl.pallas_call(
        functools.partial(_ret_kernel, chunk=chunk),
        grid=(n_seq, n_chunks),
        in_specs=[blk, blk, blk, blk, st_blk, pl.BlockSpec((1, WIDTH), lambda b, c: (0, 0))],
        out_specs=[blk, st_blk],
        out_shape=[jax.ShapeDtypeStruct((t, WIDTH), F32),
                   jax.ShapeDtypeStruct((n_seq, N_HEADS, HEAD_DIM, HEAD_DIM), F32)],
        scratch_shapes=[pltpu.VMEM((N_HEADS, chunk, chunk), F32)],
        compiler_params=_params(("arbitrary", "arbitrary")),
        name="retention",
    )(rq, rk, rv, gate, state0, g)


def _sb_kernel(pt_ref, bias_ref, qt_ref, kt_ref, vt_ref, qbd_ref, bcol_ref, kn_ref, vn_ref, ck_ref, cv_ref,
               o_ref, os_ref,
               k_s, v_s, qm_s, s_s, w_s, a_s, acc_s, off_s, kbuf, vbuf, z_s, sp_s, sums_s, vb_s, carry_s, sacc_s, sem):
    n_tiles = k_s.shape[0]
    rows = SB_HEADS * HEAD_DIM
    heads = range(SB_HEADS)
    group = pl.program_id(1)
    bias = [bias_ref[group * SB_HEADS + hh] for hh in heads]
    grp = lax.broadcasted_iota(jnp.int32, (SB_GROUPS, SB_TILE), 0)
    qry = lax.broadcasted_iota(jnp.int32, (SB_GROUPS, SB_TILE), 1)
    diag_bound = qry - SB_GROUP_LEN * grp

    rps, n_q, _ = qbd_ref.shape
    n_new = kn_ref.shape[1]
    n_pages = pt_ref.shape[1]
    upr = n_pages // SAMPLE_UNIT
    upr_shift = upr.bit_length() - 1
    n_units = rps * upr
    assert upr == 1 << upr_shift and n_tiles * (n_tiles + 1) // 2 >= n_units + 2
    req0 = (pl.program_id(0) * pl.num_programs(1) + group) * rps
    w = SAMPLE_TILE
    bcol = bcol_ref[...]
    nt = (((1,), (1,)), ((), ()))
    srow = lax.broadcasted_iota(jnp.int32, (w, 2 * w), 0)
    scol = lax.broadcasted_iota(jnp.int32, (w, 2 * w), 1)
    later = jnp.where((srow > scol) | (scol >= w), 1.0, 0.0).astype(BF16)

    def page_copies(u, slot):
        r = lax.shift_right_logical(u, upr_shift)
        first = n_pages - 1 - SAMPLE_UNIT * (u & (upr - 1))
        copies = []
        for i in range(SAMPLE_UNIT):
            page = pt_ref[req0 + r, first - i]
            copies.append(pltpu.make_async_copy(ck_ref.at[page], kbuf.at[slot, i], sem.at[slot]))
            copies.append(pltpu.make_async_copy(cv_ref.at[page], vbuf.at[slot, i], sem.at[slot]))
        return copies

    def tile_sums(z, mask):
        sp = _softplus(z)
        if mask is not None:
            sp = jnp.where(mask, sp, 0.0)
        lk = (-sp).astype(BF16)
        stacked = jnp.concatenate([lk[:, t * w:(t + 1) * w] for t in range(z.shape[1] // w)], axis=0)
        return sp, jnp.dot(stacked, later, preferred_element_type=F32)

    def weights(z, sp, sums, mask, carry):
        tiles = [None] * (z.shape[1] // w)
        for t in reversed(range(len(tiles))):
            s = sums[t * n_q:(t + 1) * n_q]
            tiles[t] = jnp.exp(z[:, t * w:(t + 1) * w] - sp[:, t * w:(t + 1) * w] + s[:, :w] + carry)
            carry = carry + s[:, w:]
        a = jnp.concatenate(tiles, axis=1)
        if mask is not None:
            a = jnp.where(mask, a, 0.0)
        pad_q = jnp.zeros((128 - n_q, a.shape[1]), F32)
        return jnp.concatenate([a, pad_q], axis=0).T.astype(BF16), carry

    def gather(buf, slot):
        return jnp.concatenate([buf[slot, i].reshape(WIDTH, PAGE_SIZE) for i in reversed(range(SAMPLE_UNIT))],
                               axis=1).astype(BF16)

    def last_stage_unit(u):
        return u - 2

    def sample_begin(u):
        @pl.when(u < n_units)
        def _():
            slot = u & 1

            @pl.when(u + 1 < n_units)
            def _():
                for cp in page_copies(u + 1, 1 - slot):
                    cp.start()

            for cp in page_copies(u, slot):
                cp.wait()

        uc = last_stage_unit(u)

        @pl.when((uc >= 0) & (uc < n_units) & ((uc & (upr - 1)) == 0))
        def _():
            r = lax.shift_right_logical(uc, upr_shift)
            pad = jnp.zeros((w - n_new, WIDTH), F32)
            kn = jnp.concatenate([kn_ref[r], pad], axis=0).astype(BF16)
            vnt = jnp.concatenate([vn_ref[r], pad], axis=0).T.astype(BF16)
            z = lax.dot_general(qbd_ref[r], kn, nt, preferred_element_type=F32) + bcol
            key = lax.broadcasted_iota(jnp.int32, (n_q, w), 1)
            tok = lax.broadcasted_iota(jnp.int32, (n_q, w), 0) % n_new
            mask = key < tok
            sp, sums = tile_sums(z, mask)
            at, carry = weights(z, sp, sums, mask, jnp.zeros((n_q, w), F32))
            carry_s[...] = carry
            sacc_s[...] = jnp.dot(vnt, at, preferred_element_type=F32)

    def sample_stages(u):
        r = jnp.minimum(lax.shift_right_logical(u, upr_shift), rps - 1)
        box = {}

        def scores_stage():
            ring = u & 3
            z_s[ring] = jnp.dot(qbd_ref[r], gather(kbuf, u & 1), preferred_element_type=F32) + bcol
            vb_s[ring] = gather(vbuf, u & 1)

        def sums_stage():
            ring = (u - 1) & 3
            sp, sums = tile_sums(z_s[ring], None)
            sp_s[ring] = sp
            sums_s[ring] = sums

        def weights_stage():
            ring = (u - 2) & 3
            old = carry_s[...]
            box["at"], carry = weights(z_s[ring], sp_s[ring], sums_s[ring], None, old)
            uc = jnp.full(old.shape, last_stage_unit(u), jnp.int32)
            carry_s[...] = jnp.where((uc >= 0) & (uc < n_units), carry, old)

        def values_stage(part):
            ring = (u - 2) & 3
            rows_sl = pl.ds(part * (WIDTH // SB_HEADS), WIDTH // SB_HEADS)
            old = sacc_s[rows_sl, :]
            pv = jnp.dot(vb_s[ring, rows_sl, :], box["at"], preferred_element_type=F32)
            uc = jnp.full(old.shape, last_stage_unit(u), jnp.int32)
            sacc_s[rows_sl, :] = jnp.where((uc >= 0) & (uc < n_units), old + pv, old)

        return scores_stage, sums_stage, weights_stage, values_stage

    def sample_end(u):
        uc = last_stage_unit(u)

        @pl.when((uc >= 0) & (uc < n_units) & ((uc & (upr - 1)) == upr - 1))
        def _():
            r = lax.shift_right_logical(uc, upr_shift)
            acct = sacc_s[...].T
            os_ref[r] = jnp.concatenate(
                [acct[h * n_new:(h + 1) * n_new, h * HEAD_DIM:(h + 1) * HEAD_DIM] for h in range(N_HEADS)], axis=1)

    for cp in page_copies(0, 0):
        cp.start()
    for ring_ref in (z_s, sp_s, sums_s, vb_s):
        ring_ref[...] = jnp.zeros_like(ring_ref)

    pi = lax.broadcasted_iota(jnp.int32, (SB_TILE, SB_TILE), 0)
    pk = lax.broadcasted_iota(jnp.int32, (SB_TILE, SB_TILE), 1)
    perm = jnp.where(pk == (pi % SB_GROUPS) * SB_GROUP_LEN + pi // SB_GROUPS, 1.0, 0.0).astype(BF16)

    def permute(c, _):
        sl = pl.ds(pl.multiple_of(c * SB_TILE, SB_TILE), SB_TILE)
        kt = kt_ref[0, :, sl].astype(BF16)
        vt = vt_ref[0, :, sl].astype(BF16)
        k_s[c] = lax.dot_general(perm, kt, nt, preferred_element_type=F32).astype(BF16)
        v_s[c] = lax.dot_general(vt, perm, nt, preferred_element_type=F32).astype(BF16)
        return 0

    lax.fori_loop(0, n_tiles, permute, 0)

    def scores(c, hh):
        s_s[hh] = jnp.dot(k_s[c], qm_s[hh], preferred_element_type=F32)

    def weighted_values(c, hh):
        return jnp.dot(v_s[c, hh * HEAD_DIM:(hh + 1) * HEAD_DIM, :], a_s[hh], preferred_element_type=F32)

    def pass1(hh, carry, masked):
        run = jnp.zeros((SB_GROUPS, SB_TILE), F32)
        for j in reversed(range(SB_GROUP_LEN)):
            sl = slice(SB_GROUPS * j, SB_GROUPS * (j + 1))
            z = s_s[hh, sl, :] + bias[hh]
            sp = _softplus(z)
            if masked:
                sp = jnp.where(diag_bound > j, sp, 0.0)
            run = run - sp
            w_s[hh, sl, :] = z + run
        scan = run
        for sh in (1, 2, 4):
            scan = scan + jnp.where(grp + sh < SB_GROUPS, pltpu.roll(scan, SB_GROUPS - sh, 0), 0.0)
        offset = scan - run + carry
        return offset, jnp.broadcast_to(scan[0:1, :], carry.shape) + carry

    def pass2(hh, offset, masked):
        for m in range(SB_GROUP_LEN // 2):
            pair = []
            for j in (2 * m, 2 * m + 1):
                a = jnp.exp(w_s[hh, SB_GROUPS * j:SB_GROUPS * (j + 1), :] + offset)
                if masked:
                    a = jnp.where(diag_bound > j, a, 0.0)
                pair.append(a)
            a_s[hh, 2 * SB_GROUPS * m:2 * SB_GROUPS * (m + 1), :] = jnp.concatenate(pair, axis=0).astype(BF16)

    def tile_iteration(u, nxt, carries, masked):
        scores_stage, sums_stage, weights_stage, values_stage = sample_stages(u)
        weights_stage()
        state = []
        for hh in heads:
            values_stage(hh)
            state.append(pass1(hh, carries[hh], masked))
            scores(nxt, hh)
        for hh in heads:
            off_s[hh] = state[hh][0]

        @pl.when(u >= 0)
        def _():
            sums_stage()
            scores_stage()
            for hh in heads:
                pass2(hh, off_s[hh], masked)

        return tuple(st[1] for st in state)

    head_of_row = lax.broadcasted_iota(jnp.int32, (rows, SB_TILE), 0) // HEAD_DIM

    def qblock(qb, u):
        qt = qt_ref[0, :, pl.ds(pl.multiple_of(qb * SB_TILE, SB_TILE), SB_TILE)]
        for hh in heads:
            qm_s[hh] = jnp.where(head_of_row == hh, qt, jnp.zeros_like(qt))
        acc_s[...] = jnp.zeros_like(acc_s)
        sample_begin(u)
        for hh in heads:
            scores(qb, hh)
        zero_c = jnp.zeros((SB_GROUPS, SB_TILE), F32)
        carries = tile_iteration(u, jnp.maximum(qb - 1, 0), (zero_c,) * SB_HEADS, True)
        sample_end(u)

        def body(i, cu):
            carries, u = cu
            c = qb - 1 - i
            sample_begin(u)
            pv = [weighted_values(c + 1, hh) for hh in heads]
            carries = tile_iteration(u, jnp.maximum(c - 1, 0), carries, False)
            for hh in heads:
                acc_s[hh] += pv[hh]
            sample_end(u)
            return carries, u + 1

        _, u = lax.fori_loop(0, qb, body, (carries, u + 1))
        for hh in heads:
            o_ref[0, hh, qb] = acc_s[hh] + weighted_values(0, hh)
        return u

    lax.fori_loop(0, n_tiles, qblock, jnp.int32(0))


def _sb(bias, qt, kt, vt, page_table, qbd, bias_col, k_new, v_new, cache_k, cache_v):
    b, _, seq = qt.shape
    n_tiles = seq // SB_TILE
    rows = SB_HEADS * HEAD_DIM
    groups = N_HEADS // SB_HEADS
    n_req, n_q, _ = qbd.shape
    n_new = k_new.shape[1]
    rps = n_req // (b * groups)
    page_shape = (2, SAMPLE_UNIT, N_HEADS, HEAD_DIM, PAGE_SIZE)
    unit_keys = SAMPLE_UNIT * PAGE_SIZE
    in_blk = pl.BlockSpec((1, rows, seq), lambda i, g, pt: (i, g, 0))
    req_map = lambda i, g, pt: (i * groups + g, 0, 0)
    grid_spec = pltpu.PrefetchScalarGridSpec(
        num_scalar_prefetch=1,
        grid=(b, groups),
        in_specs=[pl.BlockSpec(memory_space=pltpu.SMEM), in_blk, in_blk, in_blk,
                  pl.BlockSpec((rps, n_q, WIDTH), req_map),
                  pl.BlockSpec((n_q, 1), lambda i, g, pt: (0, 0)),
                  pl.BlockSpec((rps, n_new, WIDTH), req_map),
                  pl.BlockSpec((rps, n_new, WIDTH), req_map),
                  pl.BlockSpec(memory_space=pl.ANY),
                  pl.BlockSpec(memory_space=pl.ANY)],
        out_specs=[pl.BlockSpec((1, SB_HEADS, n_tiles, HEAD_DIM, SB_TILE), lambda i, g, pt: (i, g, 0, 0, 0)),
                   pl.BlockSpec((rps, n_new, WIDTH), req_map)],
        scratch_shapes=[pltpu.VMEM((n_tiles, SB_TILE, rows), BF16),
                        pltpu.VMEM((n_tiles, rows, SB_TILE), BF16),
                        pltpu.VMEM((SB_HEADS, rows, SB_TILE), BF16),
                        pltpu.VMEM((SB_HEADS, SB_TILE, SB_TILE), F32),
                        pltpu.VMEM((SB_HEADS, SB_TILE, SB_TILE), F32),
                        pltpu.VMEM((SB_HEADS, SB_TILE, SB_TILE), BF16),
                        pltpu.VMEM((SB_HEADS, HEAD_DIM, SB_TILE), F32),
                        pltpu.VMEM((SB_HEADS, SB_GROUPS, SB_TILE), F32),
                        pltpu.VMEM(page_shape, F32),
                        pltpu.VMEM(page_shape, F32),
                        pltpu.VMEM((4, n_q, unit_keys), F32),
                        pltpu.VMEM((4, n_q, unit_keys), F32),
                        pltpu.VMEM((4, n_q * unit_keys // SAMPLE_TILE, 2 * SAMPLE_TILE), F32),
                        pltpu.VMEM((4, WIDTH, unit_keys), BF16),
                        pltpu.VMEM((n_q, SAMPLE_TILE), F32),
                        pltpu.VMEM((WIDTH, 128), F32),
                        pltpu.SemaphoreType.DMA((2,))],
    )
    return pl.pallas_call(
        _sb_kernel,
        grid_spec=grid_spec,
        out_shape=[jax.ShapeDtypeStruct((b, N_HEADS, n_tiles, HEAD_DIM, SB_TILE), F32),
                   jax.ShapeDtypeStruct((n_req, n_new, WIDTH), F32)],
        compiler_params=_params(("arbitrary", "arbitrary"), _VMEM_LIMIT_SB),
        name="sb",
    )(page_table, bias, qt, kt, vt, qbd, bias_col, k_new, v_new, cache_k, cache_v)


def _out_mlp_kernel(x_ref, ret_ref, sb_ref, sbg_ref, wo_ref, g_post_ref, g_pre_ref, g_mlp_ref,
                    wu_ref, wd_ref, y_ref, *, sb_transposed):
    def rms(v, g_ref):
        return v * lax.rsqrt(jnp.mean(v * v, axis=-1, keepdims=True) + NORM_EPS) * g_ref[...]

    mix = jnp.dot(ret_ref[...].astype(BF16), wo_ref[:WIDTH, :], preferred_element_type=F32)
    if sb_transposed:
        o = sb_ref[0, :, 0]
        y = o * lax.rsqrt(jnp.mean(o * o, axis=1, keepdims=True) + NORM_EPS) * sbg_ref[...]
        sb = y.reshape(WIDTH, y.shape[-1]).T.astype(BF16)
    else:
        parts = []
        for h in range(N_HEADS):
            sl = slice(h * HEAD_DIM, (h + 1) * HEAD_DIM)
            o = sb_ref[:, sl]
            parts.append(o * lax.rsqrt(jnp.mean(o * o, axis=-1, keepdims=True) + NORM_EPS) * sbg_ref[:, sl])
        sb = jnp.concatenate(parts, axis=1).astype(BF16)
    mix = mix + jnp.dot(sb, wo_ref[WIDTH:, :], preferred_element_type=F32)
    y1 = x_ref[...] + rms(mix, g_post_ref)
    h2 = rms(y1, g_pre_ref).astype(BF16)
    u = jnp.maximum(jnp.dot(h2, wu_ref[...], preferred_element_type=F32), 0.0)
    d = jnp.dot((u * u).astype(BF16), wd_ref[...], preferred_element_type=F32)
    y_ref[...] = y1 + rms(d, g_mlp_ref)


def _out_mlp(x, ret, sb, sb_g, w_out, g_post, g_pre, g_mlp, w_up, w_down, tm, sb_transposed):
    t = x.shape[0]
    tok = lambda i: (i, 0)
    const = lambda i: (0, 0)
    if sb_transposed:
        n_tiles = sb.shape[2]
        sb_spec = pl.BlockSpec((1, N_HEADS, 1, HEAD_DIM, tm), lambda i: (i // n_tiles, 0, i % n_tiles, 0, 0))
        sbg_spec = pl.BlockSpec((N_HEADS, HEAD_DIM, 1), lambda i: (0, 0, 0))
    else:
        sb_spec = pl.BlockSpec((tm, WIDTH), tok)
        sbg_spec = pl.BlockSpec((1, WIDTH), const)
    vec = pl.BlockSpec((1, D_MODEL), const)
    return pl.pallas_call(
        functools.partial(_out_mlp_kernel, sb_transposed=sb_transposed),
        grid=(t // tm,),
        in_specs=[pl.BlockSpec((tm, D_MODEL), tok),
                  pl.BlockSpec((tm, WIDTH), tok),
                  sb_spec, sbg_spec,
                  pl.BlockSpec((2 * WIDTH, D_MODEL), const, pipeline_mode=pl.Buffered(1)),
                  vec, vec, vec,
                  pl.BlockSpec((D_MODEL, D_FF), const, pipeline_mode=pl.Buffered(1)),
                  pl.BlockSpec((D_FF, D_MODEL), const, pipeline_mode=pl.Buffered(1))],
        out_specs=pl.BlockSpec((tm, D_MODEL), tok),
        out_shape=jax.ShapeDtypeStruct((t, D_MODEL), F32),
        compiler_params=_params(("parallel",)),
        name="out_mlp",
    )(x, ret, sb, sb_g, w_out, g_post, g_pre, g_mlp, w_up, w_down)


def _rotary_tables(pos):
    half = HEAD_DIM // 2
    inv = ROPE_BASE ** (-jnp.arange(half, dtype=F32) / half)
    ang = pos[:, None] * inv[None, :]
    cos, sin = jnp.cos(ang), jnp.sin(ang)
    return jnp.tile(jnp.concatenate([cos, cos], axis=1), (1, 2)), jnp.tile(jnp.concatenate([-sin, sin], axis=1), (1, 2))


def kernel(x_prompt, x_sample, cache_sb_k, cache_sb_v, page_table, state_ret, norm_mix_pre, norm_mix_post, w_in, ret_norm_g, sb_bias, sb_norm_g, w_out, norm_mlp_pre, norm_mlp_post, w_up, w_down):
    assert w_in.shape[0] == 1, "single layer"
    batch, seq, _ = x_prompt.shape
    n_req, n_new, _ = x_sample.shape
    n_pages = page_table.shape[1]
    past_len = n_pages * PAGE_SIZE

    w_in_b = w_in[0].astype(BF16)
    w_out_b = w_out[0].astype(BF16)
    w_up_b = w_up[0].astype(BF16)
    w_down_b = w_down[0].astype(BF16)
    bias = sb_bias[0].astype(F32)
    dt = x_prompt.dtype

    tm_p = 512
    cos_p, sin_p = _rotary_tables(jnp.arange(seq, dtype=F32))
    xp = x_prompt.reshape(batch * seq, D_MODEL)
    rq, rk, rv, gate, sqt, skt, svt = _proj(xp, norm_mix_pre, w_in_b, cos_p, sin_p, tm_p, batch, True)
    zero_state = jnp.zeros((batch, N_HEADS, HEAD_DIM, HEAD_DIM), F32)
    ret_p, state_p = _retention(rq, rk, rv, gate, zero_state, ret_norm_g, batch, RET_CHUNK)

    tm_s = 256
    cos_s, sin_s = _rotary_tables(past_len + jnp.arange(n_new, dtype=F32))
    cos_s, sin_s = jnp.tile(cos_s, (tm_s // n_new, 1)), jnp.tile(sin_s, (tm_s // n_new, 1))
    xs = x_sample.reshape(n_req * n_new, D_MODEL)
    rq, rk, rv, gate, sq, sk_s, sv_s = _proj(xs, norm_mix_pre, w_in_b, cos_s, sin_s, tm_s, n_req, False)
    ret_s, state_s = _retention(rq, rk, rv, gate, state_ret[0].astype(F32), ret_norm_g, n_req, n_new)

    q4 = sq.reshape(n_req, n_new, N_HEADS, HEAD_DIM).transpose(0, 2, 1, 3)
    eye = jnp.eye(N_HEADS, dtype=BF16)
    qbd = (q4[:, :, :, None, :] * eye[None, :, None, :, None]).reshape(n_req, N_HEADS * n_new, WIDTH)
    bias_col = jnp.repeat(bias, n_new)[:, None]
    cache_k = cache_sb_k[0].transpose(0, 2, 3, 1)
    cache_v = cache_sb_v[0].transpose(0, 2, 3, 1)
    sb_p, sb_s = _sb(bias, sqt, skt, svt, page_table, qbd, bias_col, sk_s.reshape(n_req, n_new, WIDTH),
                     sv_s.reshape(n_req, n_new, WIDTH), cache_k, cache_v)

    y_p = _out_mlp(xp, ret_p, sb_p, sb_norm_g.reshape(N_HEADS, HEAD_DIM, 1), w_out_b, norm_mix_post,
                   norm_mlp_pre, norm_mlp_post, w_up_b, w_down_b, SB_TILE, True)
    y_s = _out_mlp(xs, ret_s, sb_s.reshape(n_req * n_new, WIDTH), sb_norm_g, w_out_b, norm_mix_post,
                   norm_mlp_pre, norm_mlp_post, w_up_b, w_down_b, tm_s, False)
    k_prompt = skt.reshape(1, batch, N_HEADS, HEAD_DIM, seq).transpose(0, 1, 4, 2, 3)
    v_prompt = svt.reshape(1, batch, N_HEADS, HEAD_DIM, seq).transpose(0, 1, 4, 2, 3)

    return (y_p.reshape(batch, seq, D_MODEL).astype(dt),
            y_s.reshape(n_req, n_new, D_MODEL).astype(dt),
            k_prompt.astype(dt),
            v_prompt.astype(dt),
            state_p[None].astype(dt),
            sk_s.reshape(1, n_req, n_new, N_HEADS, HEAD_DIM).astype(cache_sb_k.dtype),
            sv_s.reshape(1, n_req, n_new, N_HEADS, HEAD_DIM).astype(cache_sb_v.dtype),
            state_s[None].astype(state_ret.dtype))
```

```python
import functools

import numpy as np
import jax
import jax.numpy as jnp
from jax import lax
from jax.experimental import pallas as pl
from jax.experimental.pallas import tpu as pltpu

F32 = jnp.float32
BF16 = jnp.bfloat16

D_MODEL = 1024
HEAD_DIM = 64
N_HEADS = 8
WIDTH = N_HEADS * HEAD_DIM
N_SEG = 7
D_FF = 4 * D_MODEL
RET_CHUNK = 128
PAGE_SIZE = 128
ROPE_BASE = 10000.0
NORM_EPS = 1e-6
QK_SCALE = HEAD_DIM ** -0.5
LOG2E = float(np.log2(np.e))

SB_TILE = 256
SB_GROUPS = 8
SB_GROUP_LEN = SB_TILE // SB_GROUPS
SB_HEADS = 4
SAMPLE_UNIT = 8
SAMPLE_TILE = 2 * PAGE_SIZE
RING = 4
PAGE_PREFETCH = 2

_LOG_GAMMA = [float(np.log1p(-np.exp2(np.float32(-5.0 - h)), dtype=np.float32)) for h in range(N_HEADS)]

_VMEM_LIMIT = 56 * 1024 * 1024


def _params(semantics):
    return pltpu.CompilerParams(dimension_semantics=semantics, vmem_limit_bytes=_VMEM_LIMIT)


def _softplus(z):
    return jnp.maximum(z, 0.0) + jnp.log(1.0 + jnp.exp2(jnp.abs(z) * -LOG2E))


def _proj_kernel(x_ref, g_ref, w_ref, cos_ref, sin_ref,
                 rq_ref, rk_ref, rv_ref, gate_ref, sq_ref, sk_ref, sv_ref, *perm_refs, sb_transposed):
    x = x_ref[...]
    h = x * lax.rsqrt(jnp.mean(x * x, axis=-1, keepdims=True) + NORM_EPS) * g_ref[...]
    hb = h.astype(BF16)

    def seg(i):
        return jnp.dot(hb, w_ref[:, i * WIDTH:(i + 1) * WIDTH], preferred_element_type=F32)

    cos = jnp.concatenate([cos_ref[...]] * 4, axis=1)
    sin = jnp.concatenate([sin_ref[...]] * 4, axis=1)
    lane = lax.broadcasted_iota(jnp.int32, cos.shape, 1)
    first_half = (lane % HEAD_DIM) < (HEAD_DIM // 2)

    def rotary(p):
        swapped = jnp.where(first_half, pltpu.roll(p, WIDTH - HEAD_DIM // 2, 1), pltpu.roll(p, HEAD_DIM // 2, 1))
        return p * cos + swapped * sin

    rq_ref[...] = rotary(seg(0))
    rk_ref[...] = rotary(seg(1)) * QK_SCALE
    rv_ref[...] = seg(2)
    gate_ref[...] = seg(3)
    if sb_transposed:
        sq_ref[0] = (seg(4) * QK_SCALE).T.astype(BF16)
        k = seg(5)
        vt = seg(6).T
        sk_ref[0] = k.T
        sv_ref[0] = vt
        kp_ref, vpt_ref = perm_refs
        pi = lax.broadcasted_iota(jnp.int32, (SB_TILE, SB_TILE), 0)
        pk = lax.broadcasted_iota(jnp.int32, (SB_TILE, SB_TILE), 1)
        perm = jnp.where(pk == (pi % SB_GROUPS) * SB_GROUP_LEN + pi // SB_GROUPS, 1.0, 0.0).astype(BF16)
        kb = k.astype(BF16)
        vtb = vt.astype(BF16)
        for t in range(k.shape[0] // SB_TILE):
            sl = slice(t * SB_TILE, (t + 1) * SB_TILE)
            kp_ref[sl, :] = jnp.dot(perm, kb[sl], preferred_element_type=F32).astype(BF16)
            vpt_ref[0, :, sl] = lax.dot_general(vtb[:, sl], perm, (((1,), (1,)), ((), ())),
                                                preferred_element_type=F32).astype(BF16)
    else:
        sq_ref[...] = (seg(4) * QK_SCALE).astype(BF16)
        sk_ref[...] = seg(5)
        sv_ref[...] = seg(6)


def _proj(x, g, w_bf16, cos_tab, sin_tab, tm, n_seq, sb_transposed):
    t = x.shape[0]
    n_tab = cos_tab.shape[0] // tm
    tok = lambda i: (i, 0)
    tab = lambda i: (i % n_tab, 0)
    const = lambda i: (0, 0)
    out_f32 = jax.ShapeDtypeStruct((t, WIDTH), F32)
    blk = pl.BlockSpec((tm, WIDTH), tok)
    if sb_transposed:
        per_seq = t // (n_seq * tm)
        sb_blk = pl.BlockSpec((1, WIDTH, tm), lambda i: (i // per_seq, 0, i % per_seq))
        sb_shape = (n_seq, WIDTH, t // n_seq)
        perm_specs = [blk, sb_blk]
        perm_shapes = [jax.ShapeDtypeStruct((t, WIDTH), BF16), jax.ShapeDtypeStruct(sb_shape, BF16)]
    else:
        sb_blk = blk
        sb_shape = (t, WIDTH)
        perm_specs, perm_shapes = [], []
    return pl.pallas_call(
        functools.partial(_proj_kernel, sb_transposed=sb_transposed),
        grid=(t // tm,),
        in_specs=[pl.BlockSpec((tm, D_MODEL), tok),
                  pl.BlockSpec((1, D_MODEL), const),
                  pl.BlockSpec((D_MODEL, N_SEG * WIDTH), const),
                  pl.BlockSpec((tm, 128), tab),
                  pl.BlockSpec((tm, 128), tab)],
        out_specs=[blk] * 4 + [sb_blk] * 3 + perm_specs,
        out_shape=[out_f32] * 4 + [jax.ShapeDtypeStruct(sb_shape, BF16), jax.ShapeDtypeStruct(sb_shape, F32),
                                   jax.ShapeDtypeStruct(sb_shape, F32)] + perm_shapes,
        compiler_params=_params(("parallel",)),
        name="proj",
    )(x, g, w_bf16, cos_tab, sin_tab)


def _ret_kernel(q_ref, k_ref, v_ref, gate_ref, s0_ref, g_ref, o_ref, s_ref, decay_s, *, chunk):
    heads = range(N_HEADS)
    sl = [slice(h * HEAD_DIM, (h + 1) * HEAD_DIM) for h in heads]

    @pl.when(pl.program_id(1) == 0)
    def _():
        s_ref[...] = s0_ref[...]
        row = lax.broadcasted_iota(jnp.int32, (chunk, chunk), 0)
        col = lax.broadcasted_iota(jnp.int32, (chunk, chunk), 1)
        diff = (row - col).astype(F32)
        for h in heads:
            decay_s[h] = jnp.where(diff >= 0, jnp.exp(_LOG_GAMMA[h] * jnp.maximum(diff, 0.0)), 0.0)

    idx = lax.broadcasted_iota(jnp.int32, (chunk, 1), 0).astype(F32)
    nt = (((1,), (1,)), ((), ()))
    tn = (((0,), (0,)), ((), ()))
    qh = [q_ref[:, sl[h]].astype(BF16) for h in heads]
    kf = [k_ref[:, sl[h]] for h in heads]
    vh = [v_ref[:, sl[h]].astype(BF16) for h in heads]
    state = [s_ref[0, h] for h in heads]
    scores = [lax.dot_general(qh[h], kf[h].astype(BF16), nt, preferred_element_type=F32) for h in heads]
    cross = [jnp.dot(qh[h], state[h].astype(BF16), preferred_element_type=F32) for h in heads]
    kd = [(kf[h] * jnp.exp(_LOG_GAMMA[h] * (chunk - 1.0 - idx))).astype(BF16) for h in heads]
    kv = [lax.dot_general(kd[h], vh[h], tn, preferred_element_type=F32) for h in heads]
    intra = [jnp.dot((scores[h] * decay_s[h]).astype(BF16), vh[h], preferred_element_type=F32) for h in heads]
    for h in heads:
        lg = _LOG_GAMMA[h]
        s_ref[0, h] = jnp.exp(jnp.full((1, 1), lg * chunk, F32)) * state[h] + kv[h]
        o = intra[h] + cross[h] * jnp.exp(lg * (idx + 1.0))
        o = o - jnp.mean(o, axis=-1, keepdims=True)
        y = o * lax.rsqrt(jnp.mean(o * o, axis=-1, keepdims=True) + NORM_EPS)
        gate = gate_ref[:, sl[h]]
        o_ref[:, sl[h]] = y * g_ref[:, sl[h]] * (gate / (1.0 + jnp.exp(-gate)))


def _retention(rq, rk, rv, gate, state0, g, n_seq, chunk):
    t = rq.shape[0]
    n_chunks = t // (n_seq * chunk)
    tok = lambda b, c: (b * n_chunks + c, 0)
    st = lambda b, c: (b, 0, 0, 0)
    blk = pl.BlockSpec((chunk, WIDTH), tok)
    st_blk = pl.BlockSpec((1, N_HEADS, HEAD_DIM, HEAD_DIM), st)
    return pl.pallas_call(
        functools.partial(_ret_kernel, chunk=chunk),
        grid=(n_seq, n_chunks),
        in_specs=[blk, blk, blk, blk, st_blk, pl.BlockSpec((1, WIDTH), lambda b, c: (0, 0))],
        out_specs=[blk, st_blk],
        out_shape=[jax.ShapeDtypeStruct((t, WIDTH), F32),
                   jax.ShapeDtypeStruct((n_seq, N_HEADS, HEAD_DIM, HEAD_DIM), F32)],
        scratch_shapes=[pltpu.VMEM((N_HEADS, chunk, chunk), F32)],
        compiler_params=_params(("arbitrary", "arbitrary")),
        name="retention",
    )(rq, rk, rv, gate, state0, g)


def _sb_kernel(pt_ref, bias_ref, qt_ref, kp_ref, vpt_ref, qbd_ref, bcol_ref, kn_ref, vn_ref, ck_ref, cv_ref,
               o_ref, os_ref,
               qm_s, s_s, w_s, a_s, acc_s, off_s, kbuf, vbuf, z_s, sp_s, sums_s, vb_s, carry_s, sacc_s, sem):
    n_tiles = qt_ref.shape[2] // SB_TILE
    rows = SB_HEADS * HEAD_DIM
    heads = range(SB_HEADS)
    group = pl.program_id(1)
    bias = [bias_ref[group * SB_HEADS + hh] for hh in heads]
    grp = lax.broadcasted_iota(jnp.int32, (SB_GROUPS, SB_TILE), 0)
    qry = lax.broadcasted_iota(jnp.int32, (SB_GROUPS, SB_TILE), 1)
    diag_bound = qry - SB_GROUP_LEN * grp

    rps, n_q, _ = qbd_ref.shape
    n_new = kn_ref.shape[1]
    n_pages = pt_ref.shape[1]
    upr = n_pages // SAMPLE_UNIT
    upr_shift = upr.bit_length() - 1
    n_units = rps * upr
    assert upr == 1 << upr_shift and n_tiles * (n_tiles + 1) // 2 >= n_units + 2
    req0 = (pl.program_id(0) * pl.num_programs(1) + group) * rps
    w = SAMPLE_TILE
    bcol = bcol_ref[...]
    nt = (((1,), (1,)), ((), ()))
    srow = lax.broadcasted_iota(jnp.int32, (w, 2 * w), 0)
    scol = lax.broadcasted_iota(jnp.int32, (w, 2 * w), 1)
    later = jnp.where((srow > scol) | (scol >= w), 1.0, 0.0).astype(BF16)

    def page_copies(u, slot):
        r = lax.shift_right_logical(u, upr_shift)
        first = n_pages - 1 - SAMPLE_UNIT * (u & (upr - 1))
        copies = []
        for i in range(SAMPLE_UNIT):
            page = pt_ref[req0 + r, first - i]
            copies.append(pltpu.make_async_copy(ck_ref.at[page], kbuf.at[slot, i], sem.at[slot]))
            copies.append(pltpu.make_async_copy(cv_ref.at[page], vbuf.at[slot, i], sem.at[slot]))
        return copies

    def tile_sums(z, mask):
        sp = _softplus(z)
        if mask is not None:
            sp = jnp.where(mask, sp, 0.0)
        lk = (-sp).astype(BF16)
        stacked = jnp.concatenate([lk[:, t * w:(t + 1) * w] for t in range(z.shape[1] // w)], axis=0)
        return sp, jnp.dot(stacked, later, preferred_element_type=F32)

    def weights(z, sp, sums, mask, carry):
        tiles = [None] * (z.shape[1] // w)
        for t in reversed(range(len(tiles))):
            s = sums[t * n_q:(t + 1) * n_q]
            tiles[t] = jnp.exp(z[:, t * w:(t + 1) * w] - sp[:, t * w:(t + 1) * w] + s[:, :w] + carry)
            carry = carry + s[:, w:]
        a = jnp.concatenate(tiles, axis=1)
        if mask is not None:
            a = jnp.where(mask, a, 0.0)
        pad_q = jnp.zeros((128 - n_q, a.shape[1]), F32)
        return jnp.concatenate([a, pad_q], axis=0).T.astype(BF16), carry

    def gather(buf, slot):
        return jnp.concatenate([buf[slot, i].reshape(WIDTH, PAGE_SIZE) for i in reversed(range(SAMPLE_UNIT))],
                               axis=1).astype(BF16)

    def last_stage_unit(u):
        return u - 2

    def sample_begin(u):
        @pl.when(u < n_units)
        def _():
            @pl.when(u + PAGE_PREFETCH < n_units)
            def _():
                for cp in page_copies(u + PAGE_PREFETCH, (u + PAGE_PREFETCH) & (RING - 1)):
                    cp.start()

            for cp in page_copies(u, u & (RING - 1)):
                cp.wait()

        uc = last_stage_unit(u)

        @pl.when((uc >= 0) & (uc < n_units) & ((uc & (upr - 1)) == 0))
        def _():
            r = lax.shift_right_logical(uc, upr_shift)
            pad = jnp.zeros((w - n_new, WIDTH), F32)
            kn = jnp.concatenate([kn_ref[r], pad], axis=0).astype(BF16)
            vnt = jnp.concatenate([vn_ref[r], pad], axis=0).T.astype(BF16)
            z = lax.dot_general(qbd_ref[r], kn, nt, preferred_element_type=F32) + bcol
            key = lax.broadcasted_iota(jnp.int32, (n_q, w), 1)
            tok = lax.broadcasted_iota(jnp.int32, (n_q, w), 0) % n_new
            mask = key < tok
            sp, sums = tile_sums(z, mask)
            at, carry = weights(z, sp, sums, mask, jnp.zeros((n_q, w), F32))
            carry_s[...] = carry
            sacc_s[...] = jnp.dot(vnt, at, preferred_element_type=F32)

    def sample_stages(u):
        r = jnp.minimum(lax.shift_right_logical(u, upr_shift), rps - 1)
        box = {}

        def scores_stage():
            ring = u & (RING - 1)
            z_s[ring] = jnp.dot(qbd_ref[r], gather(kbuf, ring), preferred_element_type=F32) + bcol
            vb_s[ring] = gather(vbuf, ring)

        def sums_stage():
            ring = (u - 1) & (RING - 1)
            sp, sums = tile_sums(z_s[ring], None)
            sp_s[ring] = sp
            sums_s[ring] = sums

        def weights_stage():
            ring = (u - 2) & (RING - 1)
            old = carry_s[...]
            box["at"], carry = weights(z_s[ring], sp_s[ring], sums_s[ring], None, old)
            uc = jnp.full(old.shape, last_stage_unit(u), jnp.int32)
            carry_s[...] = jnp.where((uc >= 0) & (uc < n_units), carry, old)

        def values_stage(part):
            ring = (u - 2) & (RING - 1)
            rows_sl = pl.ds(part * (WIDTH // SB_HEADS), WIDTH // SB_HEADS)
            old = sacc_s[rows_sl, :]
            pv = jnp.dot(vb_s[ring, rows_sl, :], box["at"], preferred_element_type=F32)
            uc = jnp.full(old.shape, last_stage_unit(u), jnp.int32)
            sacc_s[rows_sl, :] = jnp.where((uc >= 0) & (uc < n_units), old + pv, old)

        return scores_stage, sums_stage, weights_stage, values_stage

    def sample_end(u):
        uc = last_stage_unit(u)

        @pl.when((uc >= 0) & (uc < n_units) & ((uc & (upr - 1)) == upr - 1))
        def _():
            r = lax.shift_right_logical(uc, upr_shift)
            acct = sacc_s[...].T
            os_ref[r] = jnp.concatenate(
                [acct[h * n_new:(h + 1) * n_new, h * HEAD_DIM:(h + 1) * HEAD_DIM] for h in range(N_HEADS)], axis=1)

    for first_unit in range(PAGE_PREFETCH):
        for cp in page_copies(first_unit, first_unit):
            cp.start()
    for ring_ref in (z_s, sp_s, sums_s, vb_s):
        ring_ref[...] = jnp.zeros_like(ring_ref)

    def tile_slice(c):
        return pl.ds(pl.multiple_of(c * SB_TILE, SB_TILE), SB_TILE)

    def scores(c, hh):
        s_s[hh] = jnp.dot(kp_ref[0, tile_slice(c), :], qm_s[hh], preferred_element_type=F32)

    def weighted_values(c, hh):
        return jnp.dot(vpt_ref[0, hh * HEAD_DIM:(hh + 1) * HEAD_DIM, tile_slice(c)], a_s[hh],
                       preferred_element_type=F32)

    def pass1(hh, carry, masked):
        run = jnp.zeros((SB_GROUPS, SB_TILE), F32)
        for j in reversed(range(SB_GROUP_LEN)):
            sl = slice(SB_GROUPS * j, SB_GROUPS * (j + 1))
            z = s_s[hh, sl, :] + bias[hh]
            sp = _softplus(z)
            if masked:
                sp = jnp.where(diag_bound > j, sp, 0.0)
            run = run - sp
            w_s[hh, sl, :] = z + run
        scan = run
        for sh in (1, 2, 4):
            scan = scan + jnp.where(grp + sh < SB_GROUPS, pltpu.roll(scan, SB_GROUPS - sh, 0), 0.0)
        offset = scan - run + carry
        return offset, jnp.broadcast_to(scan[0:1, :], carry.shape) + carry

    def pass2(hh, offset, masked):
        for m in range(SB_GROUP_LEN // 2):
            pair = []
            for j in (2 * m, 2 * m + 1):
                a = jnp.exp(w_s[hh, SB_GROUPS * j:SB_GROUPS * (j + 1), :] + offset)
                if masked:
                    a = jnp.where(diag_bound > j, a, 0.0)
                pair.append(a)
            a_s[hh, 2 * SB_GROUPS * m:2 * SB_GROUPS * (m + 1), :] = jnp.concatenate(pair, axis=0).astype(BF16)

    def tile_iteration(u, nxt, carries, masked):
        scores_stage, sums_stage, weights_stage, values_stage = sample_stages(u)
        weights_stage()
        state = []
        for hh in heads:
            values_stage(hh)
            state.append(pass1(hh, carries[hh], masked))
            scores(nxt, hh)
        for hh in heads:
            off_s[hh] = state[hh][0]

        @pl.when(u >= 0)
        def _():
            sums_stage()
            scores_stage()
            for hh in heads:
                pass2(hh, off_s[hh], masked)

        return tuple(st[1] for st in state)

    head_of_row = lax.broadcasted_iota(jnp.int32, (rows, SB_TILE), 0) // HEAD_DIM

    def qblock(qb, u):
        qt = qt_ref[0, :, pl.ds(pl.multiple_of(qb * SB_TILE, SB_TILE), SB_TILE)]
        for hh in heads:
            qm_s[hh] = jnp.where(head_of_row == hh, qt, jnp.zeros_like(qt))
        acc_s[...] = jnp.zeros_like(acc_s)
        sample_begin(u)
        for hh in heads:
            scores(qb, hh)
        zero_c = jnp.zeros((SB_GROUPS, SB_TILE), F32)
        carries = tile_iteration(u, jnp.maximum(qb - 1, 0), (zero_c,) * SB_HEADS, True)
        sample_end(u)

        def body(i, cu):
            carries, u = cu
            c = qb - 1 - i
            sample_begin(u)
            pv = [weighted_values(c + 1, hh) for hh in heads]
            carries = tile_iteration(u, jnp.maximum(c - 1, 0), carries, False)
            for hh in heads:
                acc_s[hh] += pv[hh]
            sample_end(u)
            return carries, u + 1

        _, u = lax.fori_loop(0, qb, body, (carries, u + 1))
        for hh in heads:
            o_ref[0, hh, qb] = acc_s[hh] + weighted_values(0, hh)
        return u

    lax.fori_loop(0, n_tiles, qblock, jnp.int32(0))


def _sb(bias, qt, kp, vpt, page_table, qbd, bias_col, k_new, v_new, cache_k, cache_v):
    b, _, seq = qt.shape
    n_tiles = seq // SB_TILE
    rows = SB_HEADS * HEAD_DIM
    groups = N_HEADS // SB_HEADS
    n_req, n_q, _ = qbd.shape
    n_new = k_new.shape[1]
    rps = n_req // (b * groups)
    page_shape = (RING, SAMPLE_UNIT, N_HEADS, HEAD_DIM, PAGE_SIZE)
    unit_keys = SAMPLE_UNIT * PAGE_SIZE
    in_blk = pl.BlockSpec((1, rows, seq), lambda i, g, pt: (i, g, 0))
    key_blk = pl.BlockSpec((1, seq, rows), lambda i, g, pt: (i, 0, g))
    req_map = lambda i, g, pt: (i * groups + g, 0, 0)
    grid_spec = pltpu.PrefetchScalarGridSpec(
        num_scalar_prefetch=1,
        grid=(b, groups),
        in_specs=[pl.BlockSpec(memory_space=pltpu.SMEM), in_blk, key_blk, in_blk,
                  pl.BlockSpec((rps, n_q, WIDTH), req_map),
                  pl.BlockSpec((n_q, 1), lambda i, g, pt: (0, 0)),
                  pl.BlockSpec((rps, n_new, WIDTH), req_map),
                  pl.BlockSpec((rps, n_new, WIDTH), req_map),
                  pl.BlockSpec(memory_space=pl.ANY),
                  pl.BlockSpec(memory_space=pl.ANY)],
        out_specs=[pl.BlockSpec((1, SB_HEADS, n_tiles, HEAD_DIM, SB_TILE), lambda i, g, pt: (i, g, 0, 0, 0)),
                   pl.BlockSpec((rps, n_new, WIDTH), req_map)],
        scratch_shapes=[pltpu.VMEM((SB_HEADS, rows, SB_TILE), BF16),
                        pltpu.VMEM((SB_HEADS, SB_TILE, SB_TILE), F32),
                        pltpu.VMEM((SB_HEADS, SB_TILE, SB_TILE), F32),
                        pltpu.VMEM((SB_HEADS, SB_TILE, SB_TILE), BF16),
                        pltpu.VMEM((SB_HEADS, HEAD_DIM, SB_TILE), F32),
                        pltpu.VMEM((SB_HEADS, SB_GROUPS, SB_TILE), F32),
                        pltpu.VMEM(page_shape, F32),
                        pltpu.VMEM(page_shape, F32),
                        pltpu.VMEM((RING, n_q, unit_keys), F32),
                        pltpu.VMEM((RING, n_q, unit_keys), F32),
                        pltpu.VMEM((RING, n_q * unit_keys // SAMPLE_TILE, 2 * SAMPLE_TILE), F32),
                        pltpu.VMEM((RING, WIDTH, unit_keys), BF16),
                        pltpu.VMEM((n_q, SAMPLE_TILE), F32),
                        pltpu.VMEM((WIDTH, 128), F32),
                        pltpu.SemaphoreType.DMA((RING,))],
    )
    return pl.pallas_call(
        _sb_kernel,
        grid_spec=grid_spec,
        out_shape=[jax.ShapeDtypeStruct((b, N_HEADS, n_tiles, HEAD_DIM, SB_TILE), F32),
                   jax.ShapeDtypeStruct((n_req, n_new, WIDTH), F32)],
        compiler_params=_params(("arbitrary", "arbitrary")),
        name="sb",
    )(page_table, bias, qt, kp, vpt, qbd, bias_col, k_new, v_new, cache_k, cache_v)


def _out_mlp_kernel(x_ref, ret_ref, sb_ref, sbg_ref, wo_ref, g_post_ref, g_pre_ref, g_mlp_ref,
                    wu_ref, wd_ref, y_ref, *, sb_transposed):
    def rms(v, g_ref):
        return v * lax.rsqrt(jnp.mean(v * v, axis=-1, keepdims=True) + NORM_EPS) * g_ref[...]

    mix = jnp.dot(ret_ref[...].astype(BF16), wo_ref[:WIDTH, :], preferred_element_type=F32)
    if sb_transposed:
        o = sb_ref[0, :, 0]
        y = o * lax.rsqrt(jnp.mean(o * o, axis=1, keepdims=True) + NORM_EPS) * sbg_ref[...]
        sb = y.reshape(WIDTH, y.shape[-1]).T.astype(BF16)
    else:
        parts = []
        for h in range(N_HEADS):
            sl = slice(h * HEAD_DIM, (h + 1) * HEAD_DIM)
            o = sb_ref[:, sl]
            parts.append(o * lax.rsqrt(jnp.mean(o * o, axis=-1, keepdims=True) + NORM_EPS) * sbg_ref[:, sl])
        sb = jnp.concatenate(parts, axis=1).astype(BF16)
    mix = mix + jnp.dot(sb, wo_ref[WIDTH:, :], preferred_element_type=F32)
    y1 = x_ref[...] + rms(mix, g_post_ref)
    h2 = rms(y1, g_pre_ref).astype(BF16)
    u = jnp.maximum(jnp.dot(h2, wu_ref[...], preferred_element_type=F32), 0.0)
    d = jnp.dot((u * u).astype(BF16), wd_ref[...], preferred_element_type=F32)
    y_ref[...] = y1 + rms(d, g_mlp_ref)


def _out_mlp(x, ret, sb, sb_g, w_out, g_post, g_pre, g_mlp, w_up, w_down, tm, sb_transposed):
    t = x.shape[0]
    tok = lambda i: (i, 0)
    const = lambda i: (0, 0)
    if sb_transposed:
        n_tiles = sb.shape[2]
        sb_spec = pl.BlockSpec((1, N_HEADS, 1, HEAD_DIM, tm), lambda i: (i // n_tiles, 0, i % n_tiles, 0, 0))
        sbg_spec = pl.BlockSpec((N_HEADS, HEAD_DIM, 1), lambda i: (0, 0, 0))
    else:
        sb_spec = pl.BlockSpec((tm, WIDTH), tok)
        sbg_spec = pl.BlockSpec((1, WIDTH), const)
    vec = pl.BlockSpec((1, D_MODEL), const)
    return pl.pallas_call(
        functools.partial(_out_mlp_kernel, sb_transposed=sb_transposed),
        grid=(t // tm,),
        in_specs=[pl.BlockSpec((tm, D_MODEL), tok),
                  pl.BlockSpec((tm, WIDTH), tok),
                  sb_spec, sbg_spec,
                  pl.BlockSpec((2 * WIDTH, D_MODEL), const, pipeline_mode=pl.Buffered(1)),
                  vec, vec, vec,
                  pl.BlockSpec((D_MODEL, D_FF), const, pipeline_mode=pl.Buffered(1)),
                  pl.BlockSpec((D_FF, D_MODEL), const, pipeline_mode=pl.Buffered(1))],
        out_specs=pl.BlockSpec((tm, D_MODEL), tok),
        out_shape=jax.ShapeDtypeStruct((t, D_MODEL), F32),
        compiler_params=_params(("parallel",)),
        name="out_mlp",
    )(x, ret, sb, sb_g, w_out, g_post, g_pre, g_mlp, w_up, w_down)


def _rotary_tables(pos):
    half = HEAD_DIM // 2
    inv = ROPE_BASE ** (-jnp.arange(half, dtype=F32) / half)
    ang = pos[:, None] * inv[None, :]
    cos, sin = jnp.cos(ang), jnp.sin(ang)
    return jnp.tile(jnp.concatenate([cos, cos], axis=1), (1, 2)), jnp.tile(jnp.concatenate([-sin, sin], axis=1), (1, 2))


def kernel(x_prompt, x_sample, cache_sb_k, cache_sb_v, page_table, state_ret, norm_mix_pre, norm_mix_post, w_in, ret_norm_g, sb_bias, sb_norm_g, w_out, norm_mlp_pre, norm_mlp_post, w_up, w_down):
    assert w_in.shape[0] == 1, "single layer"
    batch, seq, _ = x_prompt.shape
    n_req, n_new, _ = x_sample.shape
    n_pages = page_table.shape[1]
    past_len = n_pages * PAGE_SIZE

    w_in_b = w_in[0].astype(BF16)
    w_out_b = w_out[0].astype(BF16)
    w_up_b = w_up[0].astype(BF16)
    w_down_b = w_down[0].astype(BF16)
    bias = sb_bias[0].astype(F32)
    dt = x_prompt.dtype

    tm_p = 512
    cos_p, sin_p = _rotary_tables(jnp.arange(seq, dtype=F32))
    xp = x_prompt.reshape(batch * seq, D_MODEL)
    rq, rk, rv, gate, sqt, skt, svt, kp, vpt = _proj(xp, norm_mix_pre, w_in_b, cos_p, sin_p, tm_p, batch, True)
    zero_state = jnp.zeros((batch, N_HEADS, HEAD_DIM, HEAD_DIM), F32)
    ret_p, state_p = _retention(rq, rk, rv, gate, zero_state, ret_norm_g, batch, RET_CHUNK)

    tm_s = 256
    cos_s, sin_s = _rotary_tables(past_len + jnp.arange(n_new, dtype=F32))
    cos_s, sin_s = jnp.tile(cos_s, (tm_s // n_new, 1)), jnp.tile(sin_s, (tm_s // n_new, 1))
    xs = x_sample.reshape(n_req * n_new, D_MODEL)
    rq, rk, rv, gate, sq, sk_s, sv_s = _proj(xs, norm_mix_pre, w_in_b, cos_s, sin_s, tm_s, n_req, False)
    ret_s, state_s = _retention(rq, rk, rv, gate, state_ret[0].astype(F32), ret_norm_g, n_req, n_new)

    q4 = sq.reshape(n_req, n_new, N_HEADS, HEAD_DIM).transpose(0, 2, 1, 3)
    eye = jnp.eye(N_HEADS, dtype=BF16)
    qbd = (q4[:, :, :, None, :] * eye[None, :, None, :, None]).reshape(n_req, N_HEADS * n_new, WIDTH)
    bias_col = jnp.repeat(bias, n_new)[:, None]
    cache_k = cache_sb_k[0].transpose(0, 2, 3, 1)
    cache_v = cache_sb_v[0].transpose(0, 2, 3, 1)
    sb_p, sb_s = _sb(bias, sqt, kp.reshape(batch, seq, WIDTH), vpt, page_table, qbd, bias_col,
                     sk_s.reshape(n_req, n_new, WIDTH),
                     sv_s.reshape(n_req, n_new, WIDTH), cache_k, cache_v)

    y_p = _out_mlp(xp, ret_p, sb_p, sb_norm_g.reshape(N_HEADS, HEAD_DIM, 1), w_out_b, norm_mix_post,
                   norm_mlp_pre, norm_mlp_post, w_up_b, w_down_b, SB_TILE, True)
    y_s = _out_mlp(xs, ret_s, sb_s.reshape(n_req * n_new, WIDTH), sb_norm_g, w_out_b, norm_mix_post,
                   norm_mlp_pre, norm_mlp_post, w_up_b, w_down_b, tm_s, False)
    k_prompt = skt.reshape(1, batch, N_HEADS, HEAD_DIM, seq).transpose(0, 1, 4, 2, 3)
    v_prompt = svt.reshape(1, batch, N_HEADS, HEAD_DIM, seq).transpose(0, 1, 4, 2, 3)

    return (y_p.reshape(batch, seq, D_MODEL).astype(dt),
            y_s.reshape(n_req, n_new, D_MODEL).astype(dt),
            k_prompt.astype(dt),
            v_prompt.astype(dt),
            state_p[None].astype(dt),
            sk_s.reshape(1, n_req, n_new, N_HEADS, HEAD_DIM).astype(cache_sb_k.dtype),
            sv_s.reshape(1, n_req, n_new, N_HEADS, HEAD_DIM).astype(cache_sb_v.dtype),
            state_s[None].astype(state_ret.dtype))
```

```python
import functools

import numpy as np
import jax
import jax.numpy as jnp
from jax import lax
from jax.experimental import pallas as pl
from jax.experimental.pallas import tpu as pltpu

F32 = jnp.float32
BF16 = jnp.bfloat16

D_MODEL = 1024
HEAD_DIM = 64
N_HEADS = 8
WIDTH = N_HEADS * HEAD_DIM
N_SEG = 7
D_FF = 4 * D_MODEL
RET_CHUNK = 128
RET_UNROLL = 4
RET_SAMPLE_BATCH = 16
PAGE_SIZE = 128
ROPE_BASE = 10000.0
NORM_EPS = 1e-6
QK_SCALE = HEAD_DIM ** -0.5
LOG2E = float(np.log2(np.e))

SB_TILE = 256
SB_GROUPS = 8
SB_GROUP_LEN = SB_TILE // SB_GROUPS
SB_HEADS = 4
SAMPLE_UNIT = 8
SAMPLE_TILE = 2 * PAGE_SIZE
RING = 4
PAGE_PREFETCH = 3

_LOG_GAMMA = [float(np.log1p(-np.exp2(np.float32(-5.0 - h)), dtype=np.float32)) for h in range(N_HEADS)]

_VMEM_LIMIT = 56 * 1024 * 1024


def _params(semantics):
    return pltpu.CompilerParams(dimension_semantics=semantics, vmem_limit_bytes=_VMEM_LIMIT)


def _softplus(z):
    return jnp.maximum(z, 0.0) + jnp.log(1.0 + jnp.exp2(jnp.abs(z) * -LOG2E))


def _proj_kernel(x_ref, g_ref, w_ref, cos_ref, sin_ref,
                 rq_ref, rk_ref, rv_ref, gate_ref, sq_ref, sk_ref, sv_ref, *perm_refs, sb_transposed):
    x = x_ref[...]
    h = x * lax.rsqrt(jnp.mean(x * x, axis=-1, keepdims=True) + NORM_EPS) * g_ref[...]
    hb = h.astype(BF16)

    def seg(i):
        return jnp.dot(hb, w_ref[:, i * WIDTH:(i + 1) * WIDTH], preferred_element_type=F32)

    cos = jnp.concatenate([cos_ref[...]] * 4, axis=1)
    sin = jnp.concatenate([sin_ref[...]] * 4, axis=1)
    lane = lax.broadcasted_iota(jnp.int32, cos.shape, 1)
    first_half = (lane % HEAD_DIM) < (HEAD_DIM // 2)

    def rotary(p):
        swapped = jnp.where(first_half, pltpu.roll(p, WIDTH - HEAD_DIM // 2, 1), pltpu.roll(p, HEAD_DIM // 2, 1))
        return p * cos + swapped * sin

    rq_ref[...] = rotary(seg(0))
    rk_ref[...] = rotary(seg(1)) * QK_SCALE
    rv_ref[...] = seg(2)
    gate_ref[...] = seg(3)
    if sb_transposed:
        sq_ref[0] = (seg(4) * QK_SCALE).T.astype(BF16)
        k = seg(5)
        vt = seg(6).T
        sk_ref[0] = k.T
        sv_ref[0] = vt
        kp_ref, vpt_ref = perm_refs
        pi = lax.broadcasted_iota(jnp.int32, (SB_TILE, SB_TILE), 0)
        pk = lax.broadcasted_iota(jnp.int32, (SB_TILE, SB_TILE), 1)
        perm = jnp.where(pk == (pi % SB_GROUPS) * SB_GROUP_LEN + pi // SB_GROUPS, 1.0, 0.0).astype(BF16)
        kb = k.astype(BF16)
        vtb = vt.astype(BF16)
        for t in range(k.shape[0] // SB_TILE):
            sl = slice(t * SB_TILE, (t + 1) * SB_TILE)
            kp_ref[sl, :] = jnp.dot(perm, kb[sl], preferred_element_type=F32).astype(BF16)
            vpt_ref[0, :, sl] = lax.dot_general(vtb[:, sl], perm, (((1,), (1,)), ((), ())),
                                                preferred_element_type=F32).astype(BF16)
    else:
        sq_ref[...] = (seg(4) * QK_SCALE).astype(BF16)
        sk_ref[...] = seg(5)
        sv_ref[...] = seg(6)


def _proj(x, g, w_bf16, cos_tab, sin_tab, tm, n_seq, sb_transposed):
    t = x.shape[0]
    n_tab = cos_tab.shape[0] // tm
    tok = lambda i: (i, 0)
    tab = lambda i: (i % n_tab, 0)
    const = lambda i: (0, 0)
    out_f32 = jax.ShapeDtypeStruct((t, WIDTH), F32)
    blk = pl.BlockSpec((tm, WIDTH), tok)
    if sb_transposed:
        per_seq = t // (n_seq * tm)
        sb_blk = pl.BlockSpec((1, WIDTH, tm), lambda i: (i // per_seq, 0, i % per_seq))
        sb_shape = (n_seq, WIDTH, t // n_seq)
        perm_specs = [blk, sb_blk]
        perm_shapes = [jax.ShapeDtypeStruct((t, WIDTH), BF16), jax.ShapeDtypeStruct(sb_shape, BF16)]
    else:
        sb_blk = blk
        sb_shape = (t, WIDTH)
        perm_specs, perm_shapes = [], []
    return pl.pallas_call(
        functools.partial(_proj_kernel, sb_transposed=sb_transposed),
        grid=(t // tm,),
        in_specs=[pl.BlockSpec((tm, D_MODEL), tok),
                  pl.BlockSpec((1, D_MODEL), const),
                  pl.BlockSpec((D_MODEL, N_SEG * WIDTH), const),
                  pl.BlockSpec((tm, 128), tab),
                  pl.BlockSpec((tm, 128), tab)],
        out_specs=[blk] * 4 + [sb_blk] * 3 + perm_specs,
        out_shape=[out_f32] * 4 + [jax.ShapeDtypeStruct(sb_shape, BF16), jax.ShapeDtypeStruct(sb_shape, F32),
                                   jax.ShapeDtypeStruct(sb_shape, F32)] + perm_shapes,
        compiler_params=_params(("parallel",)),
        name="proj",
    )(x, g, w_bf16, cos_tab, sin_tab)


def _ret_kernel(q_ref, k_ref, v_ref, gate_ref, s0_ref, g_ref, o_ref, s_ref, decay_s, *, chunk):
    heads = range(N_HEADS)
    n_seq = q_ref.shape[0]
    sl = [slice(h * HEAD_DIM, (h + 1) * HEAD_DIM) for h in heads]

    @pl.when(pl.program_id(1) == 0)
    def _():
        s_ref[...] = s0_ref[...]
        row = lax.broadcasted_iota(jnp.int32, (chunk, chunk), 0)
        col = lax.broadcasted_iota(jnp.int32, (chunk, chunk), 1)
        diff = (row - col).astype(F32)
        for h in heads:
            decay_s[h] = jnp.where(diff >= 0, jnp.exp(_LOG_GAMMA[h] * jnp.maximum(diff, 0.0)), 0.0)

    idx = lax.broadcasted_iota(jnp.int32, (chunk, 1), 0).astype(F32)
    nt = (((1,), (1,)), ((), ()))
    tn = (((0,), (0,)), ((), ()))

    def one_sequence(s):
        qh = [q_ref[s, :, sl[h]].astype(BF16) for h in heads]
        kf = [k_ref[s, :, sl[h]] for h in heads]
        vh = [v_ref[s, :, sl[h]].astype(BF16) for h in heads]
        state = [s_ref[s, h] for h in heads]
        scores = [lax.dot_general(qh[h], kf[h].astype(BF16), nt, preferred_element_type=F32) for h in heads]
        cross = [jnp.dot(qh[h], state[h].astype(BF16), preferred_element_type=F32) for h in heads]
        kd = [(kf[h] * jnp.exp(_LOG_GAMMA[h] * (chunk - 1.0 - idx))).astype(BF16) for h in heads]
        kv = [lax.dot_general(kd[h], vh[h], tn, preferred_element_type=F32) for h in heads]
        intra = [jnp.dot((scores[h] * decay_s[h]).astype(BF16), vh[h], preferred_element_type=F32) for h in heads]
        for h in heads:
            lg = _LOG_GAMMA[h]
            s_ref[s, h] = jnp.exp(jnp.full((1, 1), lg * chunk, F32)) * state[h] + kv[h]
            o = intra[h] + cross[h] * jnp.exp(lg * (idx + 1.0))
            o = o - jnp.mean(o, axis=-1, keepdims=True)
            y = o * lax.rsqrt(jnp.mean(o * o, axis=-1, keepdims=True) + NORM_EPS)
            gate = gate_ref[s, :, sl[h]]
            o_ref[s, :, sl[h]] = y * g_ref[:, sl[h]] * (gate / (1.0 + jnp.exp(-gate)))

    if n_seq <= RET_UNROLL:
        for s in range(n_seq):
            one_sequence(s)
    else:
        def body(s, _):
            one_sequence(s)
            return 0

        lax.fori_loop(0, n_seq, body, 0)


def _retention(rq, rk, rv, gate, state0, g, n_seq, chunk, seq_per_step):
    t = rq.shape[0]
    seq_len = t // n_seq
    tok = lambda b, c: (b, c, 0)
    st = lambda b, c: (b, 0, 0, 0)
    blk = pl.BlockSpec((seq_per_step, chunk, WIDTH), tok)
    st_blk = pl.BlockSpec((seq_per_step, N_HEADS, HEAD_DIM, HEAD_DIM), st)
    as_seq = lambda a: a.reshape(n_seq, seq_len, WIDTH)
    out, state = pl.pallas_call(
        functools.partial(_ret_kernel, chunk=chunk),
        grid=(n_seq // seq_per_step, seq_len // chunk),
        in_specs=[blk, blk, blk, blk, st_blk, pl.BlockSpec((1, WIDTH), lambda b, c: (0, 0))],
        out_specs=[blk, st_blk],
        out_shape=[jax.ShapeDtypeStruct((n_seq, seq_len, WIDTH), F32),
                   jax.ShapeDtypeStruct((n_seq, N_HEADS, HEAD_DIM, HEAD_DIM), F32)],
        scratch_shapes=[pltpu.VMEM((N_HEADS, chunk, chunk), F32)],
        compiler_params=_params(("arbitrary", "arbitrary")),
        name="retention",
    )(as_seq(rq), as_seq(rk), as_seq(rv), as_seq(gate), state0, g)
    return out.reshape(t, WIDTH), state


def _sb_kernel(pt_ref, bias_ref, qt_ref, kp_ref, vpt_ref, qbd_ref, bcol_ref, kn_ref, vn_ref, ck_ref, cv_ref,
               o_ref, os_ref,
               qm_s, s_s, w_s, a_s, acc_s, off_s, kbuf, vbuf, z_s, sp_s, sums_s, vb_s, carry_s, sacc_s, sem):
    n_tiles = qt_ref.shape[2] // SB_TILE
    rows = SB_HEADS * HEAD_DIM
    heads = range(SB_HEADS)
    group = pl.program_id(1)
    bias = [bias_ref[group * SB_HEADS + hh] for hh in heads]
    grp = lax.broadcasted_iota(jnp.int32, (SB_GROUPS, SB_TILE), 0)
    qry = lax.broadcasted_iota(jnp.int32, (SB_GROUPS, SB_TILE), 1)
    diag_bound = qry - SB_GROUP_LEN * grp

    rps, n_q, _ = qbd_ref.shape
    n_new = kn_ref.shape[1]
    n_pages = pt_ref.shape[1]
    upr = n_pages // SAMPLE_UNIT
    upr_shift = upr.bit_length() - 1
    n_units = rps * upr
    assert upr == 1 << upr_shift and n_tiles * (n_tiles + 1) // 2 >= n_units + 2
    req0 = (pl.program_id(0) * pl.num_programs(1) + group) * rps
    w = SAMPLE_TILE
    bcol = bcol_ref[...]
    nt = (((1,), (1,)), ((), ()))
    srow = lax.broadcasted_iota(jnp.int32, (w, 2 * w), 0)
    scol = lax.broadcasted_iota(jnp.int32, (w, 2 * w), 1)
    later = jnp.where((srow > scol) | (scol >= w), 1.0, 0.0).astype(BF16)

    def page_copies(u, slot):
        r = lax.shift_right_logical(u, upr_shift)
        first = n_pages - 1 - SAMPLE_UNIT * (u & (upr - 1))
        copies = []
        for i in range(SAMPLE_UNIT):
            page = pt_ref[req0 + r, first - i]
            copies.append(pltpu.make_async_copy(ck_ref.at[page], kbuf.at[slot, i], sem.at[slot]))
            copies.append(pltpu.make_async_copy(cv_ref.at[page], vbuf.at[slot, i], sem.at[slot]))
        return copies

    def tile_sums(z, mask):
        sp = _softplus(z)
        if mask is not None:
            sp = jnp.where(mask, sp, 0.0)
        lk = (-sp).astype(BF16)
        stacked = jnp.concatenate([lk[:, t * w:(t + 1) * w] for t in range(z.shape[1] // w)], axis=0)
        return sp, jnp.dot(stacked, later, preferred_element_type=F32)

    def weights(z, sp, sums, mask, carry):
        tiles = [None] * (z.shape[1] // w)
        for t in reversed(range(len(tiles))):
            s = sums[t * n_q:(t + 1) * n_q]
            tiles[t] = jnp.exp(z[:, t * w:(t + 1) * w] - sp[:, t * w:(t + 1) * w] + s[:, :w] + carry)
            carry = carry + s[:, w:]
        a = jnp.concatenate(tiles, axis=1)
        if mask is not None:
            a = jnp.where(mask, a, 0.0)
        pad_q = jnp.zeros((128 - n_q, a.shape[1]), F32)
        return jnp.concatenate([a, pad_q], axis=0).T.astype(BF16), carry

    def gather(buf, slot):
        return jnp.concatenate([buf[slot, i].reshape(WIDTH, PAGE_SIZE) for i in reversed(range(SAMPLE_UNIT))],
                               axis=1).astype(BF16)

    def last_stage_unit(u):
        return u - 2

    def sample_begin(u):
        @pl.when(u < n_units)
        def _():
            @pl.when(u + PAGE_PREFETCH < n_units)
            def _():
                for cp in page_copies(u + PAGE_PREFETCH, (u + PAGE_PREFETCH) & (RING - 1)):
                    cp.start()

            for cp in page_copies(u, u & (RING - 1)):
                cp.wait()

        uc = last_stage_unit(u)

        @pl.when((uc >= 0) & (uc < n_units) & ((uc & (upr - 1)) == 0))
        def _():
            r = lax.shift_right_logical(uc, upr_shift)
            pad = jnp.zeros((w - n_new, WIDTH), F32)
            kn = jnp.concatenate([kn_ref[r], pad], axis=0).astype(BF16)
            vnt = jnp.concatenate([vn_ref[r], pad], axis=0).T.astype(BF16)
            z = lax.dot_general(qbd_ref[r], kn, nt, preferred_element_type=F32) + bcol
            key = lax.broadcasted_iota(jnp.int32, (n_q, w), 1)
            tok = lax.broadcasted_iota(jnp.int32, (n_q, w), 0) % n_new
            mask = key < tok
            sp, sums = tile_sums(z, mask)
            at, carry = weights(z, sp, sums, mask, jnp.zeros((n_q, w), F32))
            carry_s[...] = carry
            sacc_s[...] = jnp.dot(vnt, at, preferred_element_type=F32)

    def sample_stages(u):
        r = jnp.minimum(lax.shift_right_logical(u, upr_shift), rps - 1)
        box = {}

        def scores_stage():
            ring = u & (RING - 1)
            z_s[ring] = jnp.dot(qbd_ref[r], gather(kbuf, ring), preferred_element_type=F32) + bcol

        def cast_stage():
            ring = u & (RING - 1)
            vb_s[ring] = gather(vbuf, ring)

        def sums_stage():
            ring = (u - 1) & (RING - 1)
            sp, sums = tile_sums(z_s[ring], None)
            sp_s[ring] = sp
            sums_s[ring] = sums

        def weights_stage():
            ring = (u - 2) & (RING - 1)
            old = carry_s[...]
            box["at"], carry = weights(z_s[ring], sp_s[ring], sums_s[ring], None, old)
            uc = jnp.full(old.shape, last_stage_unit(u), jnp.int32)
            carry_s[...] = jnp.where((uc >= 0) & (uc < n_units), carry, old)

        def values_stage(part):
            ring = (u - 2) & (RING - 1)
            rows_sl = pl.ds(part * (WIDTH // SB_HEADS), WIDTH // SB_HEADS)
            old = sacc_s[rows_sl, :]
            pv = jnp.dot(vb_s[ring, rows_sl, :], box["at"], preferred_element_type=F32)
            uc = jnp.full(old.shape, last_stage_unit(u), jnp.int32)
            sacc_s[rows_sl, :] = jnp.where((uc >= 0) & (uc < n_units), old + pv, old)

        return scores_stage, cast_stage, sums_stage, weights_stage, values_stage

    def sample_end(u):
        uc = last_stage_unit(u)

        @pl.when((uc >= 0) & (uc < n_units) & ((uc & (upr - 1)) == upr - 1))
        def _():
            r = lax.shift_right_logical(uc, upr_shift)
            acct = sacc_s[...].T
            os_ref[r] = jnp.concatenate(
                [acct[h * n_new:(h + 1) * n_new, h * HEAD_DIM:(h + 1) * HEAD_DIM] for h in range(N_HEADS)], axis=1)

    for first_unit in range(PAGE_PREFETCH):
        for cp in page_copies(first_unit, first_unit):
            cp.start()
    for ring_ref in (z_s, sp_s, sums_s, vb_s):
        ring_ref[...] = jnp.zeros_like(ring_ref)

    def tile_slice(c):
        return pl.ds(pl.multiple_of(c * SB_TILE, SB_TILE), SB_TILE)

    def scores(c, hh):
        s_s[hh] = jnp.dot(kp_ref[0, tile_slice(c), :], qm_s[hh], preferred_element_type=F32)

    def weighted_values(c, hh):
        return jnp.dot(vpt_ref[0, hh * HEAD_DIM:(hh + 1) * HEAD_DIM, tile_slice(c)], a_s[hh],
                       preferred_element_type=F32)

    def pass1(hh, carry, masked):
        run = jnp.zeros((SB_GROUPS, SB_TILE), F32)
        for j in reversed(range(SB_GROUP_LEN)):
            sl = slice(SB_GROUPS * j, SB_GROUPS * (j + 1))
            z = s_s[hh, sl, :] + bias[hh]
            sp = _softplus(z)
            if masked:
                sp = jnp.where(diag_bound > j, sp, 0.0)
            run = run - sp
            w_s[hh, sl, :] = z + run
        scan = run
        for sh in (1, 2, 4):
            scan = scan + jnp.where(grp + sh < SB_GROUPS, pltpu.roll(scan, SB_GROUPS - sh, 0), 0.0)
        offset = scan - run + carry
        return offset, jnp.broadcast_to(scan[0:1, :], carry.shape) + carry

    def pass2(hh, offset, masked):
        for m in range(SB_GROUP_LEN // 2):
            pair = []
            for j in (2 * m, 2 * m + 1):
                a = jnp.exp(w_s[hh, SB_GROUPS * j:SB_GROUPS * (j + 1), :] + offset)
                if masked:
                    a = jnp.where(diag_bound > j, a, 0.0)
                pair.append(a)
            a_s[hh, 2 * SB_GROUPS * m:2 * SB_GROUPS * (m + 1), :] = jnp.concatenate(pair, axis=0).astype(BF16)

    def tile_iteration(u, nxt, carries, masked):
        scores_stage, cast_stage, sums_stage, weights_stage, values_stage = sample_stages(u)
        sample_work = [[weights_stage], [lambda: values_stage(0), lambda: values_stage(1)],
                       [lambda: values_stage(2), lambda: values_stage(3)], []]
        state = []
        for hh in heads:
            state.append(pass1(hh, carries[hh], masked))
            scores(nxt, hh)
            for work in sample_work[hh]:
                work()
        for hh in heads:
            off_s[hh] = state[hh][0]

        @pl.when(u >= 0)
        def _():
            sums_stage()
            scores_stage()
            cast_stage()
            for hh in heads:
                pass2(hh, off_s[hh], masked)

        return tuple(st[1] for st in state)

    head_of_row = lax.broadcasted_iota(jnp.int32, (rows, SB_TILE), 0) // HEAD_DIM

    def qblock(qb, u):
        qt = qt_ref[0, :, pl.ds(pl.multiple_of(qb * SB_TILE, SB_TILE), SB_TILE)]
        for hh in heads:
            qm_s[hh] = jnp.where(head_of_row == hh, qt, jnp.zeros_like(qt))
        acc_s[...] = jnp.zeros_like(acc_s)
        sample_begin(u)
        for hh in heads:
            scores(qb, hh)
        zero_c = jnp.zeros((SB_GROUPS, SB_TILE), F32)
        carries = tile_iteration(u, jnp.maximum(qb - 1, 0), (zero_c,) * SB_HEADS, True)
        sample_end(u)

        def body(i, cu):
            carries, u = cu
            c = qb - 1 - i
            sample_begin(u)
            pv = [weighted_values(c + 1, hh) for hh in heads]
            carries = tile_iteration(u, jnp.maximum(c - 1, 0), carries, False)
            for hh in heads:
                acc_s[hh] += pv[hh]
            sample_end(u)
            return carries, u + 1

        _, u = lax.fori_loop(0, qb, body, (carries, u + 1))
        for hh in heads:
            o_ref[0, hh, qb] = acc_s[hh] + weighted_values(0, hh)
        return u

    lax.fori_loop(0, n_tiles, qblock, jnp.int32(0))


def _sb(bias, qt, kp, vpt, page_table, qbd, bias_col, k_new, v_new, cache_k, cache_v):
    b, _, seq = qt.shape
    n_tiles = seq // SB_TILE
    rows = SB_HEADS * HEAD_DIM
    groups = N_HEADS // SB_HEADS
    n_req, n_q, _ = qbd.shape
    n_new = k_new.shape[1]
    rps = n_req // (b * groups)
    page_shape = (RING, SAMPLE_UNIT, N_HEADS, HEAD_DIM, PAGE_SIZE)
    unit_keys = SAMPLE_UNIT * PAGE_SIZE
    in_blk = pl.BlockSpec((1, rows, seq), lambda i, g, pt: (i, g, 0))
    key_blk = pl.BlockSpec((1, seq, rows), lambda i, g, pt: (i, 0, g))
    req_map = lambda i, g, pt: (i * groups + g, 0, 0)
    grid_spec = pltpu.PrefetchScalarGridSpec(
        num_scalar_prefetch=1,
        grid=(b, groups),
        in_specs=[pl.BlockSpec(memory_space=pltpu.SMEM), in_blk, key_blk, in_blk,
                  pl.BlockSpec((rps, n_q, WIDTH), req_map),
                  pl.BlockSpec((n_q, 1), lambda i, g, pt: (0, 0)),
                  pl.BlockSpec((rps, n_new, WIDTH), req_map),
                  pl.BlockSpec((rps, n_new, WIDTH), req_map),
                  pl.BlockSpec(memory_space=pl.ANY),
                  pl.BlockSpec(memory_space=pl.ANY)],
        out_specs=[pl.BlockSpec((1, SB_HEADS, n_tiles, HEAD_DIM, SB_TILE), lambda i, g, pt: (i, g, 0, 0, 0)),
                   pl.BlockSpec((rps, n_new, WIDTH), req_map)],
        scratch_shapes=[pltpu.VMEM((SB_HEADS, rows, SB_TILE), BF16),
                        pltpu.VMEM((SB_HEADS, SB_TILE, SB_TILE), F32),
                        pltpu.VMEM((SB_HEADS, SB_TILE, SB_TILE), F32),
                        pltpu.VMEM((SB_HEADS, SB_TILE, SB_TILE), BF16),
                        pltpu.VMEM((SB_HEADS, HEAD_DIM, SB_TILE), F32),
                        pltpu.VMEM((SB_HEADS, SB_GROUPS, SB_TILE), F32),
                        pltpu.VMEM(page_shape, F32),
                        pltpu.VMEM(page_shape, F32),
                        pltpu.VMEM((RING, n_q, unit_keys), F32),
                        pltpu.VMEM((RING, n_q, unit_keys), F32),
                        pltpu.VMEM((RING, n_q * unit_keys // SAMPLE_TILE, 2 * SAMPLE_TILE), F32),
                        pltpu.VMEM((RING, WIDTH, unit_keys), BF16),
                        pltpu.VMEM((n_q, SAMPLE_TILE), F32),
                        pltpu.VMEM((WIDTH, 128), F32),
                        pltpu.SemaphoreType.DMA((RING,))],
    )
    return pl.pallas_call(
        _sb_kernel,
        grid_spec=grid_spec,
        out_shape=[jax.ShapeDtypeStruct((b, N_HEADS, n_tiles, HEAD_DIM, SB_TILE), F32),
                   jax.ShapeDtypeStruct((n_req, n_new, WIDTH), F32)],
        compiler_params=_params(("arbitrary", "arbitrary")),
        name="sb",
    )(page_table, bias, qt, kp, vpt, qbd, bias_col, k_new, v_new, cache_k, cache_v)


def _out_mlp_kernel(x_ref, ret_ref, sb_ref, sbg_ref, wo_ref, g_post_ref, g_pre_ref, g_mlp_ref,
                    wu_ref, wd_ref, y_ref, *, sb_transposed):
    def rms(v, g_ref):
        return v * lax.rsqrt(jnp.mean(v * v, axis=-1, keepdims=True) + NORM_EPS) * g_ref[...]

    mix = jnp.dot(ret_ref[...].astype(BF16), wo_ref[:WIDTH, :], preferred_element_type=F32)
    if sb_transposed:
        o = sb_ref[0, :, 0]
        y = o * lax.rsqrt(jnp.mean(o * o, axis=1, keepdims=True) + NORM_EPS) * sbg_ref[...]
        sb = y.reshape(WIDTH, y.shape[-1]).T.astype(BF16)
    else:
        parts = []
        for h in range(N_HEADS):
            sl = slice(h * HEAD_DIM, (h + 1) * HEAD_DIM)
            o = sb_ref[:, sl]
            parts.append(o * lax.rsqrt(jnp.mean(o * o, axis=-1, keepdims=True) + NORM_EPS) * sbg_ref[:, sl])
        sb = jnp.concatenate(parts, axis=1).astype(BF16)
    mix = mix + jnp.dot(sb, wo_ref[WIDTH:, :], preferred_element_type=F32)
    y1 = x_ref[...] + rms(mix, g_post_ref)
    h2 = rms(y1, g_pre_ref).astype(BF16)
    u = jnp.maximum(jnp.dot(h2, wu_ref[...], preferred_element_type=F32), 0.0)
    d = jnp.dot((u * u).astype(BF16), wd_ref[...], preferred_element_type=F32)
    y_ref[...] = y1 + rms(d, g_mlp_ref)


def _out_mlp(x, ret, sb, sb_g, w_out, g_post, g_pre, g_mlp, w_up, w_down, tm, sb_transposed):
    t = x.shape[0]
    tok = lambda i: (i, 0)
    const = lambda i: (0, 0)
    if sb_transposed:
        n_tiles = sb.shape[2]
        sb_spec = pl.BlockSpec((1, N_HEADS, 1, HEAD_DIM, tm), lambda i: (i // n_tiles, 0, i % n_tiles, 0, 0))
        sbg_spec = pl.BlockSpec((N_HEADS, HEAD_DIM, 1), lambda i: (0, 0, 0))
    else:
        sb_spec = pl.BlockSpec((tm, WIDTH), tok)
        sbg_spec = pl.BlockSpec((1, WIDTH), const)
    vec = pl.BlockSpec((1, D_MODEL), const)
    return pl.pallas_call(
        functools.partial(_out_mlp_kernel, sb_transposed=sb_transposed),
        grid=(t // tm,),
        in_specs=[pl.BlockSpec((tm, D_MODEL), tok),
                  pl.BlockSpec((tm, WIDTH), tok),
                  sb_spec, sbg_spec,
                  pl.BlockSpec((2 * WIDTH, D_MODEL), const, pipeline_mode=pl.Buffered(1)),
                  vec, vec, vec,
                  pl.BlockSpec((D_MODEL, D_FF), const, pipeline_mode=pl.Buffered(1)),
                  pl.BlockSpec((D_FF, D_MODEL), const, pipeline_mode=pl.Buffered(1))],
        out_specs=pl.BlockSpec((tm, D_MODEL), tok),
        out_shape=jax.ShapeDtypeStruct((t, D_MODEL), F32),
        compiler_params=_params(("parallel",)),
        name="out_mlp",
    )(x, ret, sb, sb_g, w_out, g_post, g_pre, g_mlp, w_up, w_down)


def _rotary_tables(pos):
    half = HEAD_DIM // 2
    inv = ROPE_BASE ** (-jnp.arange(half, dtype=F32) / half)
    ang = pos[:, None] * inv[None, :]
    cos, sin = jnp.cos(ang), jnp.sin(ang)
    return jnp.tile(jnp.concatenate([cos, cos], axis=1), (1, 2)), jnp.tile(jnp.concatenate([-sin, sin], axis=1), (1, 2))


def kernel(x_prompt, x_sample, cache_sb_k, cache_sb_v, page_table, state_ret, norm_mix_pre, norm_mix_post, w_in, ret_norm_g, sb_bias, sb_norm_g, w_out, norm_mlp_pre, norm_mlp_post, w_up, w_down):
    assert w_in.shape[0] == 1, "single layer"
    batch, seq, _ = x_prompt.shape
    n_req, n_new, _ = x_sample.shape
    n_pages = page_table.shape[1]
    past_len = n_pages * PAGE_SIZE

    w_in_b = w_in[0].astype(BF16)
    w_out_b = w_out[0].astype(BF16)
    w_up_b = w_up[0].astype(BF16)
    w_down_b = w_down[0].astype(BF16)
    bias = sb_bias[0].astype(F32)
    dt = x_prompt.dtype

    tm_p = 512
    cos_p, sin_p = _rotary_tables(jnp.arange(seq, dtype=F32))
    xp = x_prompt.reshape(batch * seq, D_MODEL)
    rq, rk, rv, gate, sqt, skt, svt, kp, vpt = _proj(xp, norm_mix_pre, w_in_b, cos_p, sin_p, tm_p, batch, True)
    zero_state = jnp.zeros((batch, N_HEADS, HEAD_DIM, HEAD_DIM), F32)
    ret_p, state_p = _retention(rq, rk, rv, gate, zero_state, ret_norm_g, batch, RET_CHUNK, 1)

    tm_s = 256
    cos_s, sin_s = _rotary_tables(past_len + jnp.arange(n_new, dtype=F32))
    cos_s, sin_s = jnp.tile(cos_s, (tm_s // n_new, 1)), jnp.tile(sin_s, (tm_s // n_new, 1))
    xs = x_sample.reshape(n_req * n_new, D_MODEL)
    rq, rk, rv, gate, sq, sk_s, sv_s = _proj(xs, norm_mix_pre, w_in_b, cos_s, sin_s, tm_s, n_req, False)
    ret_s, state_s = _retention(rq, rk, rv, gate, state_ret[0].astype(F32), ret_norm_g, n_req, n_new, RET_SAMPLE_BATCH)

    q4 = sq.reshape(n_req, n_new, N_HEADS, HEAD_DIM).transpose(0, 2, 1, 3)
    eye = jnp.eye(N_HEADS, dtype=BF16)
    qbd = (q4[:, :, :, None, :] * eye[None, :, None, :, None]).reshape(n_req, N_HEADS * n_new, WIDTH)
    bias_col = jnp.repeat(bias, n_new)[:, None]
    cache_k = cache_sb_k[0].transpose(0, 2, 3, 1)
    cache_v = cache_sb_v[0].transpose(0, 2, 3, 1)
    sb_p, sb_s = _sb(bias, sqt, kp.reshape(batch, seq, WIDTH), vpt, page_table, qbd, bias_col,
                     sk_s.reshape(n_req, n_new, WIDTH),
                     sv_s.reshape(n_req, n_new, WIDTH), cache_k, cache_v)

    y_p = _out_mlp(xp, ret_p, sb_p, sb_norm_g.reshape(N_HEADS, HEAD_DIM, 1), w_out_b, norm_mix_post,
                   norm_mlp_pre, norm_mlp_post, w_up_b, w_down_b, SB_TILE, True)
    y_s = _out_mlp(xs, ret_s, sb_s.reshape(n_req * n_new, WIDTH), sb_norm_g, w_out_b, norm_mix_post,
                   norm_mlp_pre, norm_mlp_post, w_up_b, w_down_b, tm_s, False)
    k_prompt = skt.reshape(1, batch, N_HEADS, HEAD_DIM, seq).transpose(0, 1, 4, 2, 3)
    v_prompt = svt.reshape(1, batch, N_HEADS, HEAD_DIM, seq).transpose(0, 1, 4, 2, 3)

    return (y_p.reshape(batch, seq, D_MODEL).astype(dt),
            y_s.reshape(n_req, n_new, D_MODEL).astype(dt),
            k_prompt.astype(dt),
            v_prompt.astype(dt),
            state_p[None].astype(dt),
            sk_s.reshape(1, n_req, n_new, N_HEADS, HEAD_DIM).astype(cache_sb_k.dtype),
            sv_s.reshape(1, n_req, n_new, N_HEADS, HEAD_DIM).astype(cache_sb_v.dtype),
            state_s[None].astype(state_ret.dtype))
```

```python
import functools

import numpy as np
import jax
import jax.numpy as jnp
from jax import lax
from jax.experimental import pallas as pl
from jax.experimental.pallas import tpu as pltpu

F32 = jnp.float32
BF16 = jnp.bfloat16

D_MODEL = 1024
HEAD_DIM = 64
N_HEADS = 8
WIDTH = N_HEADS * HEAD_DIM
N_SEG = 7
D_FF = 4 * D_MODEL
RET_CHUNK = 128
RET_UNROLL = 4
RET_SAMPLE_BATCH = 16
PAGE_SIZE = 128
ROPE_BASE = 10000.0
NORM_EPS = 1e-6
QK_SCALE = HEAD_DIM ** -0.5
LOG2E = float(np.log2(np.e))

SB_TILE = 256
SB_GROUPS = 8
SB_GROUP_LEN = SB_TILE // SB_GROUPS
SB_HEADS = 4
SAMPLE_UNIT = 8
SAMPLE_TILE = 2 * PAGE_SIZE
RING = 4
PAGE_PREFETCH = 2

_LOG_GAMMA = [float(np.log1p(-np.exp2(np.float32(-5.0 - h)), dtype=np.float32)) for h in range(N_HEADS)]

_VMEM_LIMIT = 56 * 1024 * 1024


def _params(semantics):
    return pltpu.CompilerParams(dimension_semantics=semantics, vmem_limit_bytes=_VMEM_LIMIT)


def _softplus(z):
    return jnp.maximum(z, 0.0) + jnp.log(1.0 + jnp.exp2(jnp.abs(z) * -LOG2E))


def _proj_kernel(x_ref, g_ref, w_ref, cos_ref, sin_ref,
                 rq_ref, rk_ref, rv_ref, gate_ref, sq_ref, sk_ref, sv_ref, *perm_refs, sb_transposed):
    x = x_ref[...]
    h = x * lax.rsqrt(jnp.mean(x * x, axis=-1, keepdims=True) + NORM_EPS) * g_ref[...]
    hb = h.astype(BF16)

    def seg(i):
        return jnp.dot(hb, w_ref[:, i * WIDTH:(i + 1) * WIDTH], preferred_element_type=F32)

    cos = jnp.concatenate([cos_ref[...]] * 4, axis=1)
    sin = jnp.concatenate([sin_ref[...]] * 4, axis=1)
    lane = lax.broadcasted_iota(jnp.int32, cos.shape, 1)
    first_half = (lane % HEAD_DIM) < (HEAD_DIM // 2)

    def rotary(p):
        swapped = jnp.where(first_half, pltpu.roll(p, WIDTH - HEAD_DIM // 2, 1), pltpu.roll(p, HEAD_DIM // 2, 1))
        return p * cos + swapped * sin

    rq_ref[...] = rotary(seg(0))
    rk_ref[...] = rotary(seg(1)) * QK_SCALE
    rv_ref[...] = seg(2)
    gate_ref[...] = seg(3)
    if sb_transposed:
        sq_ref[0] = (seg(4) * QK_SCALE).T.astype(BF16)
        k = seg(5)
        vt = seg(6).T
        sk_ref[0] = k.T
        sv_ref[0] = vt
        kp_ref, vpt_ref = perm_refs
        pi = lax.broadcasted_iota(jnp.int32, (SB_TILE, SB_TILE), 0)
        pk = lax.broadcasted_iota(jnp.int32, (SB_TILE, SB_TILE), 1)
        perm = jnp.where(pk == (pi % SB_GROUPS) * SB_GROUP_LEN + pi // SB_GROUPS, 1.0, 0.0).astype(BF16)
        kb = k.astype(BF16)
        vtb = vt.astype(BF16)
        for t in range(k.shape[0] // SB_TILE):
            sl = slice(t * SB_TILE, (t + 1) * SB_TILE)
            kp_ref[sl, :] = jnp.dot(perm, kb[sl], preferred_element_type=F32).astype(BF16)
            vpt_ref[0, :, sl] = lax.dot_general(vtb[:, sl], perm, (((1,), (1,)), ((), ())),
                                                preferred_element_type=F32).astype(BF16)
    else:
        sq_ref[...] = (seg(4) * QK_SCALE).astype(BF16)
        sk_ref[...] = seg(5)
        sv_ref[...] = seg(6)


def _proj(x, g, w_bf16, cos_tab, sin_tab, tm, n_seq, sb_transposed):
    t = x.shape[0]
    n_tab = cos_tab.shape[0] // tm
    tok = lambda i: (i, 0)
    tab = lambda i: (i % n_tab, 0)
    const = lambda i: (0, 0)
    out_f32 = jax.ShapeDtypeStruct((t, WIDTH), F32)
    blk = pl.BlockSpec((tm, WIDTH), tok)
    if sb_transposed:
        per_seq = t // (n_seq * tm)
        sb_blk = pl.BlockSpec((1, WIDTH, tm), lambda i: (i // per_seq, 0, i % per_seq))
        sb_shape = (n_seq, WIDTH, t // n_seq)
        perm_specs = [blk, sb_blk]
        perm_shapes = [jax.ShapeDtypeStruct((t, WIDTH), BF16), jax.ShapeDtypeStruct(sb_shape, BF16)]
    else:
        sb_blk = blk
        sb_shape = (t, WIDTH)
        perm_specs, perm_shapes = [], []
    return pl.pallas_call(
        functools.partial(_proj_kernel, sb_transposed=sb_transposed),
        grid=(t // tm,),
        in_specs=[pl.BlockSpec((tm, D_MODEL), tok),
                  pl.BlockSpec((1, D_MODEL), const),
                  pl.BlockSpec((D_MODEL, N_SEG * WIDTH), const),
                  pl.BlockSpec((tm, 128), tab),
                  pl.BlockSpec((tm, 128), tab)],
        out_specs=[blk] * 4 + [sb_blk] * 3 + perm_specs,
        out_shape=[out_f32] * 4 + [jax.ShapeDtypeStruct(sb_shape, BF16), jax.ShapeDtypeStruct(sb_shape, F32),
                                   jax.ShapeDtypeStruct(sb_shape, F32)] + perm_shapes,
        compiler_params=_params(("parallel",)),
        name="proj",
    )(x, g, w_bf16, cos_tab, sin_tab)


def _ret_kernel(q_ref, k_ref, v_ref, gate_ref, s0_ref, g_ref, o_ref, s_ref, decay_s, *, chunk):
    heads = range(N_HEADS)
    n_seq = q_ref.shape[0]
    sl = [slice(h * HEAD_DIM, (h + 1) * HEAD_DIM) for h in heads]

    @pl.when(pl.program_id(1) == 0)
    def _():
        s_ref[...] = s0_ref[...]
        row = lax.broadcasted_iota(jnp.int32, (chunk, chunk), 0)
        col = lax.broadcasted_iota(jnp.int32, (chunk, chunk), 1)
        diff = (row - col).astype(F32)
        for h in heads:
            decay_s[h] = jnp.where(diff >= 0, jnp.exp(_LOG_GAMMA[h] * jnp.maximum(diff, 0.0)), 0.0)

    idx = lax.broadcasted_iota(jnp.int32, (chunk, 1), 0).astype(F32)
    nt = (((1,), (1,)), ((), ()))
    tn = (((0,), (0,)), ((), ()))

    def one_sequence(s):
        qh = [q_ref[s, :, sl[h]].astype(BF16) for h in heads]
        kf = [k_ref[s, :, sl[h]] for h in heads]
        vh = [v_ref[s, :, sl[h]].astype(BF16) for h in heads]
        state = [s_ref[s, h] for h in heads]
        scores = [lax.dot_general(qh[h], kf[h].astype(BF16), nt, preferred_element_type=F32) for h in heads]
        cross = [jnp.dot(qh[h], state[h].astype(BF16), preferred_element_type=F32) for h in heads]
        kd = [(kf[h] * jnp.exp(_LOG_GAMMA[h] * (chunk - 1.0 - idx))).astype(BF16) for h in heads]
        kv = [lax.dot_general(kd[h], vh[h], tn, preferred_element_type=F32) for h in heads]
        intra = [jnp.dot((scores[h] * decay_s[h]).astype(BF16), vh[h], preferred_element_type=F32) for h in heads]
        for h in heads:
            lg = _LOG_GAMMA[h]
            s_ref[s, h] = jnp.exp(jnp.full((1, 1), lg * chunk, F32)) * state[h] + kv[h]
            o = intra[h] + cross[h] * jnp.exp(lg * (idx + 1.0))
            o = o - jnp.mean(o, axis=-1, keepdims=True)
            y = o * lax.rsqrt(jnp.mean(o * o, axis=-1, keepdims=True) + NORM_EPS)
            gate = gate_ref[s, :, sl[h]]
            o_ref[s, :, sl[h]] = y * g_ref[:, sl[h]] * (gate / (1.0 + jnp.exp(-gate)))

    if n_seq <= RET_UNROLL:
        for s in range(n_seq):
            one_sequence(s)
    else:
        def body(s, _):
            one_sequence(s)
            return 0

        lax.fori_loop(0, n_seq, body, 0)


def _retention(rq, rk, rv, gate, state0, g, n_seq, chunk, seq_per_step):
    t = rq.shape[0]
    seq_len = t // n_seq
    tok = lambda b, c: (b, c, 0)
    st = lambda b, c: (b, 0, 0, 0)
    blk = pl.BlockSpec((seq_per_step, chunk, WIDTH), tok)
    st_blk = pl.BlockSpec((seq_per_step, N_HEADS, HEAD_DIM, HEAD_DIM), st)
    as_seq = lambda a: a.reshape(n_seq, seq_len, WIDTH)
    out, state = pl.pallas_call(
        functools.partial(_ret_kernel, chunk=chunk),
        grid=(n_seq // seq_per_step, seq_len // chunk),
        in_specs=[blk, blk, blk, blk, st_blk, pl.BlockSpec((1, WIDTH), lambda b, c: (0, 0))],
        out_specs=[blk, st_blk],
        out_shape=[jax.ShapeDtypeStruct((n_seq, seq_len, WIDTH), F32),
                   jax.ShapeDtypeStruct((n_seq, N_HEADS, HEAD_DIM, HEAD_DIM), F32)],
        scratch_shapes=[pltpu.VMEM((N_HEADS, chunk, chunk), F32)],
        compiler_params=_params(("arbitrary", "arbitrary")),
        name="retention",
    )(as_seq(rq), as_seq(rk), as_seq(rv), as_seq(gate), state0, g)
    return out.reshape(t, WIDTH), state


def _sb_kernel(pt_ref, bias_ref, qt_ref, kp_ref, vpt_ref, qbd_ref, bcol_ref, kn_ref, vn_ref, ck_ref, cv_ref,
               o_ref, os_ref,
               qm_s, s_s, w_s, a_s, acc_s, off_s, kbuf, vbuf, z_s, sp_s, sums_s, vb_s, carry_s, sacc_s, sem):
    n_tiles = qt_ref.shape[2] // SB_TILE
    rows = SB_HEADS * HEAD_DIM
    heads = range(SB_HEADS)
    group = pl.program_id(1)
    bias = [bias_ref[group * SB_HEADS + hh] for hh in heads]
    grp = lax.broadcasted_iota(jnp.int32, (SB_GROUPS, SB_TILE), 0)
    qry = lax.broadcasted_iota(jnp.int32, (SB_GROUPS, SB_TILE), 1)
    diag_bound = qry - SB_GROUP_LEN * grp

    rps, n_q, _ = qbd_ref.shape
    n_new = kn_ref.shape[1]
    n_pages = pt_ref.shape[1]
    upr = n_pages // SAMPLE_UNIT
    upr_shift = upr.bit_length() - 1
    n_units = rps * upr
    assert upr == 1 << upr_shift and n_tiles * (n_tiles + 1) // 2 >= n_units + 2
    req0 = (pl.program_id(0) * pl.num_programs(1) + group) * rps
    w = SAMPLE_TILE
    bcol = bcol_ref[...]
    nt = (((1,), (1,)), ((), ()))
    srow = lax.broadcasted_iota(jnp.int32, (w, 2 * w), 0)
    scol = lax.broadcasted_iota(jnp.int32, (w, 2 * w), 1)
    later = jnp.where((srow > scol) | (scol >= w), 1.0, 0.0).astype(BF16)

    def page_copies(u, slot):
        r = lax.shift_right_logical(u, upr_shift)
        first = n_pages - 1 - SAMPLE_UNIT * (u & (upr - 1))
        copies = []
        for i in range(SAMPLE_UNIT):
            page = pt_ref[req0 + r, first - i]
            copies.append(pltpu.make_async_copy(ck_ref.at[page], kbuf.at[slot, i], sem.at[slot]))
            copies.append(pltpu.make_async_copy(cv_ref.at[page], vbuf.at[slot, i], sem.at[slot]))
        return copies

    def tile_sums(z, mask):
        sp = _softplus(z)
        if mask is not None:
            sp = jnp.where(mask, sp, 0.0)
        lk = (-sp).astype(BF16)
        stacked = jnp.concatenate([lk[:, t * w:(t + 1) * w] for t in range(z.shape[1] // w)], axis=0)
        return sp, jnp.dot(stacked, later, preferred_element_type=F32)

    def weights(z, sp, sums, mask, carry):
        tiles = [None] * (z.shape[1] // w)
        for t in reversed(range(len(tiles))):
            s = sums[t * n_q:(t + 1) * n_q]
            tiles[t] = jnp.exp(z[:, t * w:(t + 1) * w] - sp[:, t * w:(t + 1) * w] + s[:, :w] + carry)
            carry = carry + s[:, w:]
        a = jnp.concatenate(tiles, axis=1)
        if mask is not None:
            a = jnp.where(mask, a, 0.0)
        pad_q = jnp.zeros((128 - n_q, a.shape[1]), F32)
        return jnp.concatenate([a, pad_q], axis=0).T.astype(BF16), carry

    def gather(buf, slot):
        return jnp.concatenate([buf[slot, i].reshape(WIDTH, PAGE_SIZE) for i in reversed(range(SAMPLE_UNIT))],
                               axis=1).astype(BF16)

    def last_stage_unit(u):
        return u - 2

    def sample_begin(u):
        @pl.when(u < n_units)
        def _():
            @pl.when(u + PAGE_PREFETCH < n_units)
            def _():
                for cp in page_copies(u + PAGE_PREFETCH, (u + PAGE_PREFETCH) & (RING - 1)):
                    cp.start()

            for cp in page_copies(u, u & (RING - 1)):
                cp.wait()

        uc = last_stage_unit(u)

        @pl.when((uc >= 0) & (uc < n_units) & ((uc & (upr - 1)) == 0))
        def _():
            r = lax.shift_right_logical(uc, upr_shift)
            pad = jnp.zeros((w - n_new, WIDTH), F32)
            kn = jnp.concatenate([kn_ref[r], pad], axis=0).astype(BF16)
            vnt = jnp.concatenate([vn_ref[r], pad], axis=0).T.astype(BF16)
            z = lax.dot_general(qbd_ref[r], kn, nt, preferred_element_type=F32) + bcol
            key = lax.broadcasted_iota(jnp.int32, (n_q, w), 1)
            tok = lax.broadcasted_iota(jnp.int32, (n_q, w), 0) % n_new
            mask = key < tok
            sp, sums = tile_sums(z, mask)
            at, carry = weights(z, sp, sums, mask, jnp.zeros((n_q, w), F32))
            carry_s[...] = carry
            sacc_s[...] = jnp.dot(vnt, at, preferred_element_type=F32)

    def sample_stages(u):
        r = jnp.minimum(lax.shift_right_logical(u, upr_shift), rps - 1)
        box = {}

        def scores_stage():
            ring = u & (RING - 1)
            z_s[ring] = jnp.dot(qbd_ref[r], gather(kbuf, ring), preferred_element_type=F32) + bcol

        def cast_stage():
            ring = u & (RING - 1)
            vb_s[ring] = gather(vbuf, ring)

        def sums_stage():
            ring = (u - 1) & (RING - 1)
            sp, sums = tile_sums(z_s[ring], None)
            sp_s[ring] = sp
            sums_s[ring] = sums

        def weights_stage():
            ring = (u - 2) & (RING - 1)
            old = carry_s[...]
            box["at"], carry = weights(z_s[ring], sp_s[ring], sums_s[ring], None, old)
            uc = jnp.full(old.shape, last_stage_unit(u), jnp.int32)
            carry_s[...] = jnp.where((uc >= 0) & (uc < n_units), carry, old)

        def values_stage(part):
            ring = (u - 2) & (RING - 1)
            rows_sl = pl.ds(part * (WIDTH // SB_HEADS), WIDTH // SB_HEADS)
            old = sacc_s[rows_sl, :]
            pv = jnp.dot(vb_s[ring, rows_sl, :], box["at"], preferred_element_type=F32)
            uc = jnp.full(old.shape, last_stage_unit(u), jnp.int32)
            sacc_s[rows_sl, :] = jnp.where((uc >= 0) & (uc < n_units), old + pv, old)

        return scores_stage, cast_stage, sums_stage, weights_stage, values_stage

    def sample_end(u):
        uc = last_stage_unit(u)

        @pl.when((uc >= 0) & (uc < n_units) & ((uc & (upr - 1)) == upr - 1))
        def _():
            r = lax.shift_right_logical(uc, upr_shift)
            acct = sacc_s[...].T
            os_ref[r] = jnp.concatenate(
                [acct[h * n_new:(h + 1) * n_new, h * HEAD_DIM:(h + 1) * HEAD_DIM] for h in range(N_HEADS)], axis=1)

    for first_unit in range(PAGE_PREFETCH):
        for cp in page_copies(first_unit, first_unit):
            cp.start()
    for ring_ref in (z_s, sp_s, sums_s, vb_s):
        ring_ref[...] = jnp.zeros_like(ring_ref)

    def tile_slice(c):
        return pl.ds(pl.multiple_of(c * SB_TILE, SB_TILE), SB_TILE)

    def scores(c, hh):
        s_s[hh] = jnp.dot(kp_ref[0, tile_slice(c), :], qm_s[hh], preferred_element_type=F32)

    def weighted_values(c, hh):
        return jnp.dot(vpt_ref[0, hh * HEAD_DIM:(hh + 1) * HEAD_DIM, tile_slice(c)], a_s[hh],
                       preferred_element_type=F32)

    def pass1(hh, carry, masked):
        run = jnp.zeros((SB_GROUPS, SB_TILE), F32)
        for j in reversed(range(SB_GROUP_LEN)):
            sl = slice(SB_GROUPS * j, SB_GROUPS * (j + 1))
            z = s_s[hh, sl, :] + bias[hh]
            sp = _softplus(z)
            if masked:
                sp = jnp.where(diag_bound > j, sp, 0.0)
            run = run - sp
            w_s[hh, sl, :] = z + run
        scan = run
        for sh in (1, 2, 4):
            scan = scan + jnp.where(grp + sh < SB_GROUPS, pltpu.roll(scan, SB_GROUPS - sh, 0), 0.0)
        offset = scan - run + carry
        return offset, jnp.broadcast_to(scan[0:1, :], carry.shape) + carry

    def pass2(hh, offset, masked):
        for m in range(SB_GROUP_LEN // 2):
            pair = []
            for j in (2 * m, 2 * m + 1):
                a = jnp.exp(w_s[hh, SB_GROUPS * j:SB_GROUPS * (j + 1), :] + offset)
                if masked:
                    a = jnp.where(diag_bound > j, a, 0.0)
                pair.append(a)
            a_s[hh, 2 * SB_GROUPS * m:2 * SB_GROUPS * (m + 1), :] = jnp.concatenate(pair, axis=0).astype(BF16)

    def tile_iteration(u, nxt, carries, masked):
        scores_stage, cast_stage, sums_stage, weights_stage, values_stage = sample_stages(u)
        weights_stage()
        state = []
        for hh in heads:
            values_stage(hh)
            state.append(pass1(hh, carries[hh], masked))
            scores(nxt, hh)
        for hh in heads:
            off_s[hh] = state[hh][0]

        @pl.when(u >= 0)
        def _():
            sums_stage()
            scores_stage()
            cast_stage()
            for hh in heads:
                pass2(hh, off_s[hh], masked)

        return tuple(st[1] for st in state)

    head_of_row = lax.broadcasted_iota(jnp.int32, (rows, SB_TILE), 0) // HEAD_DIM

    def qblock(qb, u):
        qt = qt_ref[0, :, pl.ds(pl.multiple_of(qb * SB_TILE, SB_TILE), SB_TILE)]
        for hh in heads:
            qm_s[hh] = jnp.where(head_of_row == hh, qt, jnp.zeros_like(qt))
        acc_s[...] = jnp.zeros_like(acc_s)
        sample_begin(u)
        for hh in heads:
            scores(qb, hh)
        zero_c = jnp.zeros((SB_GROUPS, SB_TILE), F32)
        carries = tile_iteration(u, jnp.maximum(qb - 1, 0), (zero_c,) * SB_HEADS, True)
        sample_end(u)

        def body(i, cu):
            carries, u = cu
            c = qb - 1 - i
            sample_begin(u)
            pv = [weighted_values(c + 1, hh) for hh in heads]
            carries = tile_iteration(u, jnp.maximum(c - 1, 0), carries, False)
            for hh in heads:
                acc_s[hh] += pv[hh]
            sample_end(u)
            return carries, u + 1

        _, u = lax.fori_loop(0, qb, body, (carries, u + 1))
        for hh in heads:
            o_ref[0, hh, qb] = acc_s[hh] + weighted_values(0, hh)
        return u

    lax.fori_loop(0, n_tiles, qblock, jnp.int32(0))


def _sb(bias, qt, kp, vpt, page_table, qbd, bias_col, k_new, v_new, cache_k, cache_v):
    b, _, seq = qt.shape
    n_tiles = seq // SB_TILE
    rows = SB_HEADS * HEAD_DIM
    groups = N_HEADS // SB_HEADS
    n_req, n_q, _ = qbd.shape
    n_new = k_new.shape[1]
    rps = n_req // (b * groups)
    page_shape = (RING, SAMPLE_UNIT, N_HEADS, HEAD_DIM, PAGE_SIZE)
    unit_keys = SAMPLE_UNIT * PAGE_SIZE
    in_blk = pl.BlockSpec((1, rows, seq), lambda i, g, pt: (i, g, 0))
    key_blk = pl.BlockSpec((1, seq, rows), lambda i, g, pt: (i, 0, g))
    req_map = lambda i, g, pt: (i * groups + g, 0, 0)
    grid_spec = pltpu.PrefetchScalarGridSpec(
        num_scalar_prefetch=1,
        grid=(b, groups),
        in_specs=[pl.BlockSpec(memory_space=pltpu.SMEM), in_blk, key_blk, in_blk,
                  pl.BlockSpec((rps, n_q, WIDTH), req_map),
                  pl.BlockSpec((n_q, 1), lambda i, g, pt: (0, 0)),
                  pl.BlockSpec((rps, n_new, WIDTH), req_map),
                  pl.BlockSpec((rps, n_new, WIDTH), req_map),
                  pl.BlockSpec(memory_space=pl.ANY),
                  pl.BlockSpec(memory_space=pl.ANY)],
        out_specs=[pl.BlockSpec((1, SB_HEADS, n_tiles, HEAD_DIM, SB_TILE), lambda i, g, pt: (i, g, 0, 0, 0)),
                   pl.BlockSpec((rps, n_new, WIDTH), req_map)],
        scratch_shapes=[pltpu.VMEM((SB_HEADS, rows, SB_TILE), BF16),
                        pltpu.VMEM((SB_HEADS, SB_TILE, SB_TILE), F32),
                        pltpu.VMEM((SB_HEADS, SB_TILE, SB_TILE), F32),
                        pltpu.VMEM((SB_HEADS, SB_TILE, SB_TILE), BF16),
                        pltpu.VMEM((SB_HEADS, HEAD_DIM, SB_TILE), F32),
                        pltpu.VMEM((SB_HEADS, SB_GROUPS, SB_TILE), F32),
                        pltpu.VMEM(page_shape, F32),
                        pltpu.VMEM(page_shape, F32),
                        pltpu.VMEM((RING, n_q, unit_keys), F32),
                        pltpu.VMEM((RING, n_q, unit_keys), F32),
                        pltpu.VMEM((RING, n_q * unit_keys // SAMPLE_TILE, 2 * SAMPLE_TILE), F32),
                        pltpu.VMEM((RING, WIDTH, unit_keys), BF16),
                        pltpu.VMEM((n_q, SAMPLE_TILE), F32),
                        pltpu.VMEM((WIDTH, 128), F32),
                        pltpu.SemaphoreType.DMA((RING,))],
    )
    return pl.pallas_call(
        _sb_kernel,
        grid_spec=grid_spec,
        out_shape=[jax.ShapeDtypeStruct((b, N_HEADS, n_tiles, HEAD_DIM, SB_TILE), F32),
                   jax.ShapeDtypeStruct((n_req, n_new, WIDTH), F32)],
        compiler_params=_params(("arbitrary", "arbitrary")),
        name="sb",
    )(page_table, bias, qt, kp, vpt, qbd, bias_col, k_new, v_new, cache_k, cache_v)


def _out_mlp_kernel(x_ref, ret_ref, sb_ref, sbg_ref, wo_ref, g_post_ref, g_pre_ref, g_mlp_ref,
                    wu_ref, wd_ref, y_ref, *, sb_transposed):
    def rms(v, g_ref):
        return v * lax.rsqrt(jnp.mean(v * v, axis=-1, keepdims=True) + NORM_EPS) * g_ref[...]

    mix = jnp.dot(ret_ref[...].astype(BF16), wo_ref[:WIDTH, :], preferred_element_type=F32)
    if sb_transposed:
        o = sb_ref[0, :, 0]
        y = o * lax.rsqrt(jnp.mean(o * o, axis=1, keepdims=True) + NORM_EPS) * sbg_ref[...]
        sb = y.reshape(WIDTH, y.shape[-1]).T.astype(BF16)
    else:
        parts = []
        for h in range(N_HEADS):
            sl = slice(h * HEAD_DIM, (h + 1) * HEAD_DIM)
            o = sb_ref[:, sl]
            parts.append(o * lax.rsqrt(jnp.mean(o * o, axis=-1, keepdims=True) + NORM_EPS) * sbg_ref[:, sl])
        sb = jnp.concatenate(parts, axis=1).astype(BF16)
    mix = mix + jnp.dot(sb, wo_ref[WIDTH:, :], preferred_element_type=F32)
    y1 = x_ref[...] + rms(mix, g_post_ref)
    h2 = rms(y1, g_pre_ref).astype(BF16)
    u = jnp.maximum(jnp.dot(h2, wu_ref[...], preferred_element_type=F32), 0.0)
    d = jnp.dot((u * u).astype(BF16), wd_ref[...], preferred_element_type=F32)
    y_ref[...] = y1 + rms(d, g_mlp_ref)


def _out_mlp(x, ret, sb, sb_g, w_out, g_post, g_pre, g_mlp, w_up, w_down, tm, sb_transposed):
    t = x.shape[0]
    tok = lambda i: (i, 0)
    const = lambda i: (0, 0)
    if sb_transposed:
        n_tiles = sb.shape[2]
        sb_spec = pl.BlockSpec((1, N_HEADS, 1, HEAD_DIM, tm), lambda i: (i // n_tiles, 0, i % n_tiles, 0, 0))
        sbg_spec = pl.BlockSpec((N_HEADS, HEAD_DIM, 1), lambda i: (0, 0, 0))
    else:
        sb_spec = pl.BlockSpec((tm, WIDTH), tok)
        sbg_spec = pl.BlockSpec((1, WIDTH), const)
    vec = pl.BlockSpec((1, D_MODEL), const)
    return pl.pallas_call(
        functools.partial(_out_mlp_kernel, sb_transposed=sb_transposed),
        grid=(t // tm,),
        in_specs=[pl.BlockSpec((tm, D_MODEL), tok),
                  pl.BlockSpec((tm, WIDTH), tok),
                  sb_spec, sbg_spec,
                  pl.BlockSpec((2 * WIDTH, D_MODEL), const, pipeline_mode=pl.Buffered(1)),
                  vec, vec, vec,
                  pl.BlockSpec((D_MODEL, D_FF), const, pipeline_mode=pl.Buffered(1)),
                  pl.BlockSpec((D_FF, D_MODEL), const, pipeline_mode=pl.Buffered(1))],
        out_specs=pl.BlockSpec((tm, D_MODEL), tok),
        out_shape=jax.ShapeDtypeStruct((t, D_MODEL), F32),
        compiler_params=_params(("parallel",)),
        name="out_mlp",
    )(x, ret, sb, sb_g, w_out, g_post, g_pre, g_mlp, w_up, w_down)


def _rotary_tables(pos):
    half = HEAD_DIM // 2
    inv = ROPE_BASE ** (-jnp.arange(half, dtype=F32) / half)
    ang = pos[:, None] * inv[None, :]
    cos, sin = jnp.cos(ang), jnp.sin(ang)
    return jnp.tile(jnp.concatenate([cos, cos], axis=1), (1, 2)), jnp.tile(jnp.concatenate([-sin, sin], axis=1), (1, 2))


def kernel(x_prompt, x_sample, cache_sb_k, cache_sb_v, page_table, state_ret, norm_mix_pre, norm_mix_post, w_in, ret_norm_g, sb_bias, sb_norm_g, w_out, norm_mlp_pre, norm_mlp_post, w_up, w_down):
    assert w_in.shape[0] == 1, "single layer"
    batch, seq, _ = x_prompt.shape
    n_req, n_new, _ = x_sample.shape
    n_pages = page_table.shape[1]
    past_len = n_pages * PAGE_SIZE

    w_in_b = w_in[0].astype(BF16)
    w_out_b = w_out[0].astype(BF16)
    w_up_b = w_up[0].astype(BF16)
    w_down_b = w_down[0].astype(BF16)
    bias = sb_bias[0].astype(F32)
    dt = x_prompt.dtype

    tm_p = 512
    cos_p, sin_p = _rotary_tables(jnp.arange(seq, dtype=F32))
    xp = x_prompt.reshape(batch * seq, D_MODEL)
    rq, rk, rv, gate, sqt, skt, svt, kp, vpt = _proj(xp, norm_mix_pre, w_in_b, cos_p, sin_p, tm_p, batch, True)
    zero_state = jnp.zeros((batch, N_HEADS, HEAD_DIM, HEAD_DIM), F32)
    ret_p, state_p = _retention(rq, rk, rv, gate, zero_state, ret_norm_g, batch, RET_CHUNK, 1)

    tm_s = 256
    cos_s, sin_s = _rotary_tables(past_len + jnp.arange(n_new, dtype=F32))
    cos_s, sin_s = jnp.tile(cos_s, (tm_s // n_new, 1)), jnp.tile(sin_s, (tm_s // n_new, 1))
    xs = x_sample.reshape(n_req * n_new, D_MODEL)
    rq, rk, rv, gate, sq, sk_s, sv_s = _proj(xs, norm_mix_pre, w_in_b, cos_s, sin_s, tm_s, n_req, False)
    ret_s, state_s = _retention(rq, rk, rv, gate, state_ret[0].astype(F32), ret_norm_g, n_req, n_new, RET_SAMPLE_BATCH)

    q4 = sq.reshape(n_req, n_new, N_HEADS, HEAD_DIM).transpose(0, 2, 1, 3)
    eye = jnp.eye(N_HEADS, dtype=BF16)
    qbd = (q4[:, :, :, None, :] * eye[None, :, None, :, None]).reshape(n_req, N_HEADS * n_new, WIDTH)
    bias_col = jnp.repeat(bias, n_new)[:, None]
    cache_k = cache_sb_k[0].transpose(0, 2, 3, 1)
    cache_v = cache_sb_v[0].transpose(0, 2, 3, 1)
    sb_p, sb_s = _sb(bias, sqt, kp.reshape(batch, seq, WIDTH), vpt, page_table, qbd, bias_col,
                     sk_s.reshape(n_req, n_new, WIDTH),
                     sv_s.reshape(n_req, n_new, WIDTH), cache_k, cache_v)

    y_p = _out_mlp(xp, ret_p, sb_p, sb_norm_g.reshape(N_HEADS, HEAD_DIM, 1), w_out_b, norm_mix_post,
                   norm_mlp_pre, norm_mlp_post, w_up_b, w_down_b, SB_TILE, True)
    y_s = _out_mlp(xs, ret_s, sb_s.reshape(n_req * n_new, WIDTH), sb_norm_g, w_out_b, norm_mix_post,
                   norm_mlp_pre, norm_mlp_post, w_up_b, w_down_b, tm_s, False)
    k_prompt = skt.reshape(1, batch, N_HEADS, HEAD_DIM, seq).transpose(0, 1, 4, 2, 3)
    v_prompt = svt.reshape(1, batch, N_HEADS, HEAD_DIM, seq).transpose(0, 1, 4, 2, 3)

    return (y_p.reshape(batch, seq, D_MODEL).astype(dt),
            y_s.reshape(n_req, n_new, D_MODEL).astype(dt),
            k_prompt.astype(dt),
            v_prompt.astype(dt),
            state_p[None].astype(dt),
            sk_s.reshape(1, n_req, n_new, N_HEADS, HEAD_DIM).astype(cache_sb_k.dtype),
            sv_s.reshape(1, n_req, n_new, N_HEADS, HEAD_DIM).astype(cache_sb_v.dtype),
            state_s[None].astype(state_ret.dtype))
```

```python
import functools

import numpy as np
import jax
import jax.numpy as jnp
from jax import lax
from jax.experimental import pallas as pl
from jax.experimental.pallas import tpu as pltpu

F32 = jnp.float32
BF16 = jnp.bfloat16

D_MODEL = 1024
HEAD_DIM = 64
N_HEADS = 8
WIDTH = N_HEADS * HEAD_DIM
N_SEG = 7
D_FF = 4 * D_MODEL
RET_CHUNK = 128
RET_UNROLL = 4
RET_SAMPLE_BATCH = 16
PAGE_SIZE = 128
ROPE_BASE = 10000.0
NORM_EPS = 1e-6
QK_SCALE = HEAD_DIM ** -0.5
LOG2E = float(np.log2(np.e))

SB_TILE = 256
SB_GROUPS = 8
SB_GROUP_LEN = SB_TILE // SB_GROUPS
SB_HEADS = 4
SAMPLE_UNIT = 8
SAMPLE_TILE = 2 * PAGE_SIZE
RING = 4
PAGE_PREFETCH = 2

_LOG_GAMMA = [float(np.log1p(-np.exp2(np.float32(-5.0 - h)), dtype=np.float32)) for h in range(N_HEADS)]

_VMEM_LIMIT = 56 * 1024 * 1024


def _params(semantics):
    return pltpu.CompilerParams(dimension_semantics=semantics, vmem_limit_bytes=_VMEM_LIMIT)


def _softplus(z):
    return jnp.maximum(z, 0.0) + jnp.log(1.0 + jnp.exp2(jnp.abs(z) * -LOG2E))


def _proj_kernel(x_ref, g_ref, w_ref, cos_ref, sin_ref,
                 rq_ref, rk_ref, rv_ref, gate_ref, sq_ref, sk_ref, sv_ref, *perm_refs, sb_transposed):
    x = x_ref[...]
    h = x * lax.rsqrt(jnp.mean(x * x, axis=-1, keepdims=True) + NORM_EPS) * g_ref[...]
    hb = h.astype(BF16)

    def seg(i):
        return jnp.dot(hb, w_ref[:, i * WIDTH:(i + 1) * WIDTH], preferred_element_type=F32)

    cos = jnp.concatenate([cos_ref[...]] * 4, axis=1)
    sin = jnp.concatenate([sin_ref[...]] * 4, axis=1)
    lane = lax.broadcasted_iota(jnp.int32, cos.shape, 1)
    first_half = (lane % HEAD_DIM) < (HEAD_DIM // 2)

    def rotary(p):
        swapped = jnp.where(first_half, pltpu.roll(p, WIDTH - HEAD_DIM // 2, 1), pltpu.roll(p, HEAD_DIM // 2, 1))
        return p * cos + swapped * sin

    rq_ref[...] = rotary(seg(0))
    rk_ref[...] = rotary(seg(1)) * QK_SCALE
    rv_ref[...] = seg(2)
    gate_ref[...] = seg(3)
    if sb_transposed:
        sq_ref[0] = (seg(4) * QK_SCALE).T.astype(BF16)
        k = seg(5)
        vt = seg(6).T
        sk_ref[0] = k.T
        sv_ref[0] = vt
        kp_ref, vpt_ref = perm_refs
        pi = lax.broadcasted_iota(jnp.int32, (SB_TILE, SB_TILE), 0)
        pk = lax.broadcasted_iota(jnp.int32, (SB_TILE, SB_TILE), 1)
        perm = jnp.where(pk == (pi % SB_GROUPS) * SB_GROUP_LEN + pi // SB_GROUPS, 1.0, 0.0).astype(BF16)
        kb = k.astype(BF16)
        vtb = vt.astype(BF16)
        for t in range(k.shape[0] // SB_TILE):
            sl = slice(t * SB_TILE, (t + 1) * SB_TILE)
            kp_ref[sl, :] = jnp.dot(perm, kb[sl], preferred_element_type=F32).astype(BF16)
            vpt_ref[0, :, sl] = lax.dot_general(vtb[:, sl], perm, (((1,), (1,)), ((), ())),
                                                preferred_element_type=F32).astype(BF16)
    else:
        sq_ref[...] = (seg(4) * QK_SCALE).astype(BF16)
        sk_ref[...] = seg(5)
        sv_ref[...] = seg(6)


def _proj(x, g, w_bf16, cos_tab, sin_tab, tm, n_seq, sb_transposed):
    t = x.shape[0]
    n_tab = cos_tab.shape[0] // tm
    tok = lambda i: (i, 0)
    tab = lambda i: (i % n_tab, 0)
    const = lambda i: (0, 0)
    out_f32 = jax.ShapeDtypeStruct((t, WIDTH), F32)
    blk = pl.BlockSpec((tm, WIDTH), tok)
    if sb_transposed:
        per_seq = t // (n_seq * tm)
        sb_blk = pl.BlockSpec((1, WIDTH, tm), lambda i: (i // per_seq, 0, i % per_seq))
        sb_shape = (n_seq, WIDTH, t // n_seq)
        perm_specs = [blk, sb_blk]
        perm_shapes = [jax.ShapeDtypeStruct((t, WIDTH), BF16), jax.ShapeDtypeStruct(sb_shape, BF16)]
    else:
        sb_blk = blk
        sb_shape = (t, WIDTH)
        perm_specs, perm_shapes = [], []
    return pl.pallas_call(
        functools.partial(_proj_kernel, sb_transposed=sb_transposed),
        grid=(t // tm,),
        in_specs=[pl.BlockSpec((tm, D_MODEL), tok),
                  pl.BlockSpec((1, D_MODEL), const),
                  pl.BlockSpec((D_MODEL, N_SEG * WIDTH), const),
                  pl.BlockSpec((tm, 128), tab),
                  pl.BlockSpec((tm, 128), tab)],
        out_specs=[blk] * 4 + [sb_blk] * 3 + perm_specs,
        out_shape=[out_f32] * 4 + [jax.ShapeDtypeStruct(sb_shape, BF16), jax.ShapeDtypeStruct(sb_shape, F32),
                                   jax.ShapeDtypeStruct(sb_shape, F32)] + perm_shapes,
        compiler_params=_params(("parallel",)),
        name="proj",
    )(x, g, w_bf16, cos_tab, sin_tab)


def _ret_kernel(q_ref, k_ref, v_ref, gate_ref, s0_ref, g_ref, o_ref, s_ref, decay_s, *, chunk):
    heads = range(N_HEADS)
    n_seq = q_ref.shape[0]
    sl = [slice(h * HEAD_DIM, (h + 1) * HEAD_DIM) for h in heads]

    @pl.when(pl.program_id(1) == 0)
    def _():
        s_ref[...] = s0_ref[...]
        row = lax.broadcasted_iota(jnp.int32, (chunk, chunk), 0)
        col = lax.broadcasted_iota(jnp.int32, (chunk, chunk), 1)
        diff = (row - col).astype(F32)
        for h in heads:
            decay_s[h] = jnp.where(diff >= 0, jnp.exp(_LOG_GAMMA[h] * jnp.maximum(diff, 0.0)), 0.0)

    idx = lax.broadcasted_iota(jnp.int32, (chunk, 1), 0).astype(F32)
    nt = (((1,), (1,)), ((), ()))
    tn = (((0,), (0,)), ((), ()))

    def one_sequence(s):
        qh = [q_ref[s, :, sl[h]].astype(BF16) for h in heads]
        kf = [k_ref[s, :, sl[h]] for h in heads]
        vh = [v_ref[s, :, sl[h]].astype(BF16) for h in heads]
        state = [s_ref[s, h] for h in heads]
        scores = [lax.dot_general(qh[h], kf[h].astype(BF16), nt, preferred_element_type=F32) for h in heads]
        cross = [jnp.dot(qh[h], state[h].astype(BF16), preferred_element_type=F32) for h in heads]
        kd = [(kf[h] * jnp.exp(_LOG_GAMMA[h] * (chunk - 1.0 - idx))).astype(BF16) for h in heads]
        kv = [lax.dot_general(kd[h], vh[h], tn, preferred_element_type=F32) for h in heads]
        intra = [jnp.dot((scores[h] * decay_s[h]).astype(BF16), vh[h], preferred_element_type=F32) for h in heads]
        for h in heads:
            lg = _LOG_GAMMA[h]
            s_ref[s, h] = jnp.exp(jnp.full((1, 1), lg * chunk, F32)) * state[h] + kv[h]
            o = intra[h] + cross[h] * jnp.exp(lg * (idx + 1.0))
            o = o - jnp.mean(o, axis=-1, keepdims=True)
            y = o * lax.rsqrt(jnp.mean(o * o, axis=-1, keepdims=True) + NORM_EPS)
            gate = gate_ref[s, :, sl[h]]
            o_ref[s, :, sl[h]] = y * g_ref[:, sl[h]] * (gate / (1.0 + jnp.exp(-gate)))

    if n_seq <= RET_UNROLL:
        for s in range(n_seq):
            one_sequence(s)
    else:
        def body(s, _):
            one_sequence(s)
            return 0

        lax.fori_loop(0, n_seq, body, 0)


def _retention(rq, rk, rv, gate, state0, g, n_seq, chunk, seq_per_step):
    t = rq.shape[0]
    seq_len = t // n_seq
    tok = lambda b, c: (b, c, 0)
    st = lambda b, c: (b, 0, 0, 0)
    blk = pl.BlockSpec((seq_per_step, chunk, WIDTH), tok)
    st_blk = pl.BlockSpec((seq_per_step, N_HEADS, HEAD_DIM, HEAD_DIM), st)
    as_seq = lambda a: a.reshape(n_seq, seq_len, WIDTH)
    out, state = pl.pallas_call(
        functools.partial(_ret_kernel, chunk=chunk),
        grid=(n_seq // seq_per_step, seq_len // chunk),
        in_specs=[blk, blk, blk, blk, st_blk, pl.BlockSpec((1, WIDTH), lambda b, c: (0, 0))],
        out_specs=[blk, st_blk],
        out_shape=[jax.ShapeDtypeStruct((n_seq, seq_len, WIDTH), F32),
                   jax.ShapeDtypeStruct((n_seq, N_HEADS, HEAD_DIM, HEAD_DIM), F32)],
        scratch_shapes=[pltpu.VMEM((N_HEADS, chunk, chunk), F32)],
        compiler_params=_params(("arbitrary", "arbitrary")),
        name="retention",
    )(as_seq(rq), as_seq(rk), as_seq(rv), as_seq(gate), state0, g)
    return out.reshape(t, WIDTH), state


def _sb_kernel(pt_ref, bias_ref, qt_ref, kp_ref, vpt_ref, qbd_ref, bcol_ref, kn_ref, vn_ref, ck_ref, cv_ref,
               o_ref, os_ref,
               qm_s, s_s, w_s, a_s, acc_s, off_s, kbuf, vbuf, z_s, sp_s, sums_s, vb_s, carry_s, sacc_s, sem):
    n_tiles = qt_ref.shape[2] // SB_TILE
    rows = SB_HEADS * HEAD_DIM
    heads = range(SB_HEADS)
    group = pl.program_id(1)
    bias = [bias_ref[group * SB_HEADS + hh] for hh in heads]
    grp = lax.broadcasted_iota(jnp.int32, (SB_GROUPS, SB_TILE), 0)
    qry = lax.broadcasted_iota(jnp.int32, (SB_GROUPS, SB_TILE), 1)
    diag_bound = qry - SB_GROUP_LEN * grp

    rps, n_q, _ = qbd_ref.shape
    n_new = kn_ref.shape[1]
    n_pages = pt_ref.shape[1]
    upr = n_pages // SAMPLE_UNIT
    upr_shift = upr.bit_length() - 1
    n_units = rps * upr
    assert upr == 1 << upr_shift and n_tiles * (n_tiles + 1) // 2 >= n_units + 2
    req0 = (pl.program_id(0) * pl.num_programs(1) + group) * rps
    w = SAMPLE_TILE
    bcol = bcol_ref[...]
    nt = (((1,), (1,)), ((), ()))
    srow = lax.broadcasted_iota(jnp.int32, (w, 2 * w), 0)
    scol = lax.broadcasted_iota(jnp.int32, (w, 2 * w), 1)
    later = jnp.where((srow > scol) | (scol >= w), 1.0, 0.0).astype(BF16)

    def page_copies(u, slot):
        r = lax.shift_right_logical(u, upr_shift)
        first = n_pages - 1 - SAMPLE_UNIT * (u & (upr - 1))
        copies = []
        for i in range(SAMPLE_UNIT):
            page = pt_ref[req0 + r, first - i]
            copies.append(pltpu.make_async_copy(ck_ref.at[page], kbuf.at[slot, i], sem.at[slot]))
            copies.append(pltpu.make_async_copy(cv_ref.at[page], vbuf.at[slot, i], sem.at[slot]))
        return copies

    def tile_sums(z, mask):
        sp = _softplus(z)
        if mask is not None:
            sp = jnp.where(mask, sp, 0.0)
        lk = (-sp).astype(BF16)
        stacked = jnp.concatenate([lk[:, t * w:(t + 1) * w] for t in range(z.shape[1] // w)], axis=0)
        return sp, jnp.dot(stacked, later, preferred_element_type=F32)

    def weights(z, sp, sums, mask, carry):
        tiles = [None] * (z.shape[1] // w)
        for t in reversed(range(len(tiles))):
            s = sums[t * n_q:(t + 1) * n_q]
            tiles[t] = jnp.exp(z[:, t * w:(t + 1) * w] - sp[:, t * w:(t + 1) * w] + s[:, :w] + carry)
            carry = carry + s[:, w:]
        a = jnp.concatenate(tiles, axis=1)
        if mask is not None:
            a = jnp.where(mask, a, 0.0)
        pad_q = jnp.zeros((128 - n_q, a.shape[1]), F32)
        return jnp.concatenate([a, pad_q], axis=0).T.astype(BF16), carry

    def gather(buf, slot):
        return jnp.concatenate([buf[slot, i].reshape(WIDTH, PAGE_SIZE) for i in reversed(range(SAMPLE_UNIT))],
                               axis=1).astype(BF16)

    def last_stage_unit(u):
        return u - 2

    def sample_begin(u):
        @pl.when(u < n_units)
        def _():
            @pl.when(u + PAGE_PREFETCH < n_units)
            def _():
                for cp in page_copies(u + PAGE_PREFETCH, (u + PAGE_PREFETCH) & (RING - 1)):
                    cp.start()

            for cp in page_copies(u, u & (RING - 1)):
                cp.wait()

        uc = last_stage_unit(u)

        @pl.when((uc >= 0) & (uc < n_units) & ((uc & (upr - 1)) == 0))
        def _():
            r = lax.shift_right_logical(uc, upr_shift)
            pad = jnp.zeros((w - n_new, WIDTH), F32)
            kn = jnp.concatenate([kn_ref[r], pad], axis=0).astype(BF16)
            vnt = jnp.concatenate([vn_ref[r], pad], axis=0).T.astype(BF16)
            z = lax.dot_general(qbd_ref[r], kn, nt, preferred_element_type=F32) + bcol
            key = lax.broadcasted_iota(jnp.int32, (n_q, w), 1)
            tok = lax.broadcasted_iota(jnp.int32, (n_q, w), 0) % n_new
            mask = key < tok
            sp, sums = tile_sums(z, mask)
            at, carry = weights(z, sp, sums, mask, jnp.zeros((n_q, w), F32))
            carry_s[...] = carry
            sacc_s[...] = jnp.dot(vnt, at, preferred_element_type=F32)

    def sample_stages(u):
        r = jnp.minimum(lax.shift_right_logical(u, upr_shift), rps - 1)
        box = {}

        def scores_stage():
            ring = u & (RING - 1)
            z_s[ring] = jnp.dot(qbd_ref[r], gather(kbuf, ring), preferred_element_type=F32) + bcol

        def cast_stage():
            ring = u & (RING - 1)
            vb_s[ring] = gather(vbuf, ring)

        def sums_stage():
            ring = (u - 1) & (RING - 1)
            sp, sums = tile_sums(z_s[ring], None)
            sp_s[ring] = sp
            sums_s[ring] = sums

        def weights_stage():
            ring = (u - 2) & (RING - 1)
            old = carry_s[...]
            box["at"], carry = weights(z_s[ring], sp_s[ring], sums_s[ring], None, old)
            uc = jnp.full(old.shape, last_stage_unit(u), jnp.int32)
            carry_s[...] = jnp.where((uc >= 0) & (uc < n_units), carry, old)

        def values_stage(part):
            ring = (u - 2) & (RING - 1)
            rows_sl = pl.ds(part * (WIDTH // SB_HEADS), WIDTH // SB_HEADS)
            old = sacc_s[rows_sl, :]
            pv = jnp.dot(vb_s[ring, rows_sl, :], box["at"], preferred_element_type=F32)
            uc = jnp.full(old.shape, last_stage_unit(u), jnp.int32)
            sacc_s[rows_sl, :] = jnp.where((uc >= 0) & (uc < n_units), old + pv, old)

        return scores_stage, cast_stage, sums_stage, weights_stage, values_stage

    def sample_end(u):
        uc = last_stage_unit(u)

        @pl.when((uc >= 0) & (uc < n_units) & ((uc & (upr - 1)) == upr - 1))
        def _():
            r = lax.shift_right_logical(uc, upr_shift)
            acct = sacc_s[...].T
            os_ref[r] = jnp.concatenate(
                [acct[h * n_new:(h + 1) * n_new, h * HEAD_DIM:(h + 1) * HEAD_DIM] for h in range(N_HEADS)], axis=1)

    for first_unit in range(PAGE_PREFETCH):
        for cp in page_copies(first_unit, first_unit):
            cp.start()
    for ring_ref in (z_s, sp_s, sums_s, vb_s):
        ring_ref[...] = jnp.zeros_like(ring_ref)

    def tile_slice(c):
        return pl.ds(pl.multiple_of(c * SB_TILE, SB_TILE), SB_TILE)

    def scores(c, hh):
        s_s[hh] = jnp.dot(kp_ref[0, tile_slice(c), :], qm_s[hh], preferred_element_type=F32)

    def weighted_values(c, hh):
        return jnp.dot(vpt_ref[0, hh * HEAD_DIM:(hh + 1) * HEAD_DIM, tile_slice(c)], a_s[hh],
                       preferred_element_type=F32)

    def pass1(hh, carry, masked):
        run = jnp.zeros((SB_GROUPS, SB_TILE), F32)
        for j in reversed(range(SB_GROUP_LEN)):
            sl = slice(SB_GROUPS * j, SB_GROUPS * (j + 1))
            z = s_s[hh, sl, :] + bias[hh]
            sp = _softplus(z)
            if masked:
                sp = jnp.where(diag_bound > j, sp, 0.0)
            run = run - sp
            w_s[hh, sl, :] = z + run
        scan = run
        for sh in (1, 2, 4):
            scan = scan + jnp.where(grp + sh < SB_GROUPS, pltpu.roll(scan, SB_GROUPS - sh, 0), 0.0)
        offset = scan - run + carry
        return offset, jnp.broadcast_to(scan[0:1, :], carry.shape) + carry

    def pass2(hh, offset, masked):
        for m in range(SB_GROUP_LEN // 2):
            pair = []
            for j in (2 * m, 2 * m + 1):
                a = jnp.exp(w_s[hh, SB_GROUPS * j:SB_GROUPS * (j + 1), :] + offset)
                if masked:
                    a = jnp.where(diag_bound > j, a, 0.0)
                pair.append(a)
            a_s[hh, 2 * SB_GROUPS * m:2 * SB_GROUPS * (m + 1), :] = jnp.concatenate(pair, axis=0).astype(BF16)

    def tile_iteration(u, nxt, carries, masked):
        scores_stage, cast_stage, sums_stage, weights_stage, values_stage = sample_stages(u)
        sample_work = [[weights_stage], [lambda: values_stage(0), lambda: values_stage(1)],
                       [lambda: values_stage(2), lambda: values_stage(3)], []]
        state = []
        for hh in heads:
            state.append(pass1(hh, carries[hh], masked))
            scores(nxt, hh)
            for work in sample_work[hh]:
                work()
        for hh in heads:
            off_s[hh] = state[hh][0]

        @pl.when(u >= 0)
        def _():
            sums_stage()
            scores_stage()
            cast_stage()
            for hh in heads:
                pass2(hh, off_s[hh], masked)

        return tuple(st[1] for st in state)

    head_of_row = lax.broadcasted_iota(jnp.int32, (rows, SB_TILE), 0) // HEAD_DIM

    def qblock(qb, u):
        qt = qt_ref[0, :, pl.ds(pl.multiple_of(qb * SB_TILE, SB_TILE), SB_TILE)]
        for hh in heads:
            qm_s[hh] = jnp.where(head_of_row == hh, qt, jnp.zeros_like(qt))
        acc_s[...] = jnp.zeros_like(acc_s)
        sample_begin(u)
        for hh in heads:
            scores(qb, hh)
        zero_c = jnp.zeros((SB_GROUPS, SB_TILE), F32)
        carries = tile_iteration(u, jnp.maximum(qb - 1, 0), (zero_c,) * SB_HEADS, True)
        sample_end(u)

        def body(i, cu):
            carries, u = cu
            c = qb - 1 - i
            sample_begin(u)
            pv = [weighted_values(c + 1, hh) for hh in heads]
            carries = tile_iteration(u, jnp.maximum(c - 1, 0), carries, False)
            for hh in heads:
                acc_s[hh] += pv[hh]
            sample_end(u)
            return carries, u + 1

        _, u = lax.fori_loop(0, qb, body, (carries, u + 1))
        for hh in heads:
            o_ref[0, hh, qb] = acc_s[hh] + weighted_values(0, hh)
        return u

    lax.fori_loop(0, n_tiles, qblock, jnp.int32(0))


def _sb(bias, qt, kp, vpt, page_table, qbd, bias_col, k_new, v_new, cache_k, cache_v):
    b, _, seq = qt.shape
    n_tiles = seq // SB_TILE
    rows = SB_HEADS * HEAD_DIM
    groups = N_HEADS // SB_HEADS
    n_req, n_q, _ = qbd.shape
    n_new = k_new.shape[1]
    rps = n_req // (b * groups)
    page_shape = (RING, SAMPLE_UNIT, N_HEADS, HEAD_DIM, PAGE_SIZE)
    unit_keys = SAMPLE_UNIT * PAGE_SIZE
    in_blk = pl.BlockSpec((1, rows, seq), lambda i, g, pt: (i, g, 0))
    key_blk = pl.BlockSpec((1, seq, rows), lambda i, g, pt: (i, 0, g))
    req_map = lambda i, g, pt: (i * groups + g, 0, 0)
    grid_spec = pltpu.PrefetchScalarGridSpec(
        num_scalar_prefetch=1,
        grid=(b, groups),
        in_specs=[pl.BlockSpec(memory_space=pltpu.SMEM), in_blk, key_blk, in_blk,
                  pl.BlockSpec((rps, n_q, WIDTH), req_map),
                  pl.BlockSpec((n_q, 1), lambda i, g, pt: (0, 0)),
                  pl.BlockSpec((rps, n_new, WIDTH), req_map),
                  pl.BlockSpec((rps, n_new, WIDTH), req_map),
                  pl.BlockSpec(memory_space=pl.ANY),
                  pl.BlockSpec(memory_space=pl.ANY)],
        out_specs=[pl.BlockSpec((1, SB_HEADS, n_tiles, HEAD_DIM, SB_TILE), lambda i, g, pt: (i, g, 0, 0, 0)),
                   pl.BlockSpec((rps, n_new, WIDTH), req_map)],
        scratch_shapes=[pltpu.VMEM((SB_HEADS, rows, SB_TILE), BF16),
                        pltpu.VMEM((SB_HEADS, SB_TILE, SB_TILE), F32),
                        pltpu.VMEM((SB_HEADS, SB_TILE, SB_TILE), F32),
                        pltpu.VMEM((SB_HEADS, SB_TILE, SB_TILE), BF16),
                        pltpu.VMEM((SB_HEADS, HEAD_DIM, SB_TILE), F32),
                        pltpu.VMEM((SB_HEADS, SB_GROUPS, SB_TILE), F32),
                        pltpu.VMEM(page_shape, F32),
                        pltpu.VMEM(page_shape, F32),
                        pltpu.VMEM((RING, n_q, unit_keys), F32),
                        pltpu.VMEM((RING, n_q, unit_keys), F32),
                        pltpu.VMEM((RING, n_q * unit_keys // SAMPLE_TILE, 2 * SAMPLE_TILE), F32),
                        pltpu.VMEM((RING, WIDTH, unit_keys), BF16),
                        pltpu.VMEM((n_q, SAMPLE_TILE), F32),
                        pltpu.VMEM((WIDTH, 128), F32),
                        pltpu.SemaphoreType.DMA((RING,))],
    )
    return pl.pallas_call(
        _sb_kernel,
        grid_spec=grid_spec,
        out_shape=[jax.ShapeDtypeStruct((b, N_HEADS, n_tiles, HEAD_DIM, SB_TILE), F32),
                   jax.ShapeDtypeStruct((n_req, n_new, WIDTH), F32)],
        compiler_params=_params(("arbitrary", "arbitrary")),
        name="sb",
    )(page_table, bias, qt, kp, vpt, qbd, bias_col, k_new, v_new, cache_k, cache_v)


def _out_mlp_kernel(x_ref, ret_ref, sb_ref, sbg_ref, wo_ref, g_post_ref, g_pre_ref, g_mlp_ref,
                    wu_ref, wd_ref, y_ref, *, sb_transposed):
    def rms(v, g_ref):
        return v * lax.rsqrt(jnp.mean(v * v, axis=-1, keepdims=True) + NORM_EPS) * g_ref[...]

    mix = jnp.dot(ret_ref[...].astype(BF16), wo_ref[:WIDTH, :], preferred_element_type=F32)
    if sb_transposed:
        o = sb_ref[0, :, 0]
        y = o * lax.rsqrt(jnp.mean(o * o, axis=1, keepdims=True) + NORM_EPS) * sbg_ref[...]
        sb = y.reshape(WIDTH, y.shape[-1]).T.astype(BF16)
    else:
        parts = []
        for h in range(N_HEADS):
            sl = slice(h * HEAD_DIM, (h + 1) * HEAD_DIM)
            o = sb_ref[:, sl]
            parts.append(o * lax.rsqrt(jnp.mean(o * o, axis=-1, keepdims=True) + NORM_EPS) * sbg_ref[:, sl])
        sb = jnp.concatenate(parts, axis=1).astype(BF16)
    mix = mix + jnp.dot(sb, wo_ref[WIDTH:, :], preferred_element_type=F32)
    y1 = x_ref[...] + rms(mix, g_post_ref)
    h2 = rms(y1, g_pre_ref).astype(BF16)
    u = jnp.maximum(jnp.dot(h2, wu_ref[...], preferred_element_type=F32), 0.0)
    d = jnp.dot((u * u).astype(BF16), wd_ref[...], preferred_element_type=F32)
    y_ref[...] = y1 + rms(d, g_mlp_ref)


def _out_mlp(x, ret, sb, sb_g, w_out, g_post, g_pre, g_mlp, w_up, w_down, tm, sb_transposed):
    t = x.shape[0]
    tok = lambda i: (i, 0)
    const = lambda i: (0, 0)
    if sb_transposed:
        n_tiles = sb.shape[2]
        sb_spec = pl.BlockSpec((1, N_HEADS, 1, HEAD_DIM, tm), lambda i: (i // n_tiles, 0, i % n_tiles, 0, 0))
        sbg_spec = pl.BlockSpec((N_HEADS, HEAD_DIM, 1), lambda i: (0, 0, 0))
    else:
        sb_spec = pl.BlockSpec((tm, WIDTH), tok)
        sbg_spec = pl.BlockSpec((1, WIDTH), const)
    vec = pl.BlockSpec((1, D_MODEL), const)
    return pl.pallas_call(
        functools.partial(_out_mlp_kernel, sb_transposed=sb_transposed),
        grid=(t // tm,),
        in_specs=[pl.BlockSpec((tm, D_MODEL), tok),
                  pl.BlockSpec((tm, WIDTH), tok),
                  sb_spec, sbg_spec,
                  pl.BlockSpec((2 * WIDTH, D_MODEL), const, pipeline_mode=pl.Buffered(1)),
                  vec, vec, vec,
                  pl.BlockSpec((D_MODEL, D_FF), const, pipeline_mode=pl.Buffered(1)),
                  pl.BlockSpec((D_FF, D_MODEL), const, pipeline_mode=pl.Buffered(1))],
        out_specs=pl.BlockSpec((tm, D_MODEL), tok),
        out_shape=jax.ShapeDtypeStruct((t, D_MODEL), F32),
        compiler_params=_params(("parallel",)),
        name="out_mlp",
    )(x, ret, sb, sb_g, w_out, g_post, g_pre, g_mlp, w_up, w_down)


def _rotary_tables(pos):
    half = HEAD_DIM // 2
    inv = ROPE_BASE ** (-jnp.arange(half, dtype=F32) / half)
    ang = pos[:, None] * inv[None, :]
    cos, sin = jnp.cos(ang), jnp.sin(ang)
    return jnp.tile(jnp.concatenate([cos, cos], axis=1), (1, 2)), jnp.tile(jnp.concatenate([-sin, sin], axis=1), (1, 2))


def kernel(x_prompt, x_sample, cache_sb_k, cache_sb_v, page_table, state_ret, norm_mix_pre, norm_mix_post, w_in, ret_norm_g, sb_bias, sb_norm_g, w_out, norm_mlp_pre, norm_mlp_post, w_up, w_down):
    assert w_in.shape[0] == 1, "single layer"
    batch, seq, _ = x_prompt.shape
    n_req, n_new, _ = x_sample.shape
    n_pages = page_table.shape[1]
    past_len = n_pages * PAGE_SIZE

    w_in_b = w_in[0].astype(BF16)
    w_out_b = w_out[0].astype(BF16)
    w_up_b = w_up[0].astype(BF16)
    w_down_b = w_down[0].astype(BF16)
    bias = sb_bias[0].astype(F32)
    dt = x_prompt.dtype

    tm_p = 512
    cos_p, sin_p = _rotary_tables(jnp.arange(seq, dtype=F32))
    xp = x_prompt.reshape(batch * seq, D_MODEL)
    rq, rk, rv, gate, sqt, skt, svt, kp, vpt = _proj(xp, norm_mix_pre, w_in_b, cos_p, sin_p, tm_p, batch, True)
    zero_state = jnp.zeros((batch, N_HEADS, HEAD_DIM, HEAD_DIM), F32)
    ret_p, state_p = _retention(rq, rk, rv, gate, zero_state, ret_norm_g, batch, RET_CHUNK, 1)

    tm_s = 256
    cos_s, sin_s = _rotary_tables(past_len + jnp.arange(n_new, dtype=F32))
    cos_s, sin_s = jnp.tile(cos_s, (tm_s // n_new, 1)), jnp.tile(sin_s, (tm_s // n_new, 1))
    xs = x_sample.reshape(n_req * n_new, D_MODEL)
    rq, rk, rv, gate, sq, sk_s, sv_s = _proj(xs, norm_mix_pre, w_in_b, cos_s, sin_s, tm_s, n_req, False)
    ret_s, state_s = _retention(rq, rk, rv, gate, state_ret[0].astype(F32), ret_norm_g, n_req, n_new, RET_SAMPLE_BATCH)

    q4 = sq.reshape(n_req, n_new, N_HEADS, HEAD_DIM).transpose(0, 2, 1, 3)
    eye = jnp.eye(N_HEADS, dtype=BF16)
    qbd = (q4[:, :, :, None, :] * eye[None, :, None, :, None]).reshape(n_req, N_HEADS * n_new, WIDTH)
    bias_col = jnp.repeat(bias, n_new)[:, None]
    cache_k = cache_sb_k[0].transpose(0, 2, 3, 1)
    cache_v = cache_sb_v[0].transpose(0, 2, 3, 1)
    sb_p, sb_s = _sb(bias, sqt, kp.reshape(batch, seq, WIDTH), vpt, page_table, qbd, bias_col,
                     sk_s.reshape(n_req, n_new, WIDTH),
                     sv_s.reshape(n_req, n_new, WIDTH), cache_k, cache_v)

    y_p = _out_mlp(xp, ret_p, sb_p, sb_norm_g.reshape(N_HEADS, HEAD_DIM, 1), w_out_b, norm_mix_post,
                   norm_mlp_pre, norm_mlp_post, w_up_b, w_down_b, SB_TILE, True)
    y_s = _out_mlp(xs, ret_s, sb_s.reshape(n_req * n_new, WIDTH), sb_norm_g, w_out_b, norm_mix_post,
                   norm_mlp_pre, norm_mlp_post, w_up_b, w_down_b, tm_s, False)
    k_prompt = skt.reshape(1, batch, N_HEADS, HEAD_DIM, seq).transpose(0, 1, 4, 2, 3)
    v_prompt = svt.reshape(1, batch, N_HEADS, HEAD_DIM, seq).transpose(0, 1, 4, 2, 3)

    return (y_p.reshape(batch, seq, D_MODEL).astype(dt),
            y_s.reshape(n_req, n_new, D_MODEL).astype(dt),
            k_prompt.astype(dt),
            v_prompt.astype(dt),
            state_p[None].astype(dt),
            sk_s.reshape(1, n_req, n_new, N_HEADS, HEAD_DIM).astype(cache_sb_k.dtype),
            sv_s.reshape(1, n_req, n_new, N_HEADS, HEAD_DIM).astype(cache_sb_v.dtype),
            state_s[None].astype(state_ret.dtype))
```

```python
import functools

import numpy as np
import jax
import jax.numpy as jnp
from jax import lax
from jax.experimental import pallas as pl
from jax.experimental.pallas import tpu as pltpu

F32 = jnp.float32
BF16 = jnp.bfloat16

D_MODEL = 1024
HEAD_DIM = 64
N_HEADS = 8
WIDTH = N_HEADS * HEAD_DIM
N_SEG = 7
D_FF = 4 * D_MODEL
RET_CHUNK = 128
RET_UNROLL = 4
RET_SAMPLE_BATCH = 16
RET_INNER_BATCH = 2
PAGE_SIZE = 128
ROPE_BASE = 10000.0
NORM_EPS = 1e-6
QK_SCALE = HEAD_DIM ** -0.5
LOG2E = float(np.log2(np.e))

SB_TILE = 256
SB_GROUPS = 8
SB_GROUP_LEN = SB_TILE // SB_GROUPS
SB_HEADS = 4
SAMPLE_UNIT = 8
SAMPLE_TILE = 2 * PAGE_SIZE
RING = 4
PAGE_PREFETCH = 2

_LOG_GAMMA = [float(np.log1p(-np.exp2(np.float32(-5.0 - h)), dtype=np.float32)) for h in range(N_HEADS)]

_VMEM_LIMIT = 56 * 1024 * 1024


def _params(semantics):
    return pltpu.CompilerParams(dimension_semantics=semantics, vmem_limit_bytes=_VMEM_LIMIT)


def _softplus(z):
    return jnp.maximum(z, 0.0) + jnp.log(1.0 + jnp.exp2(jnp.abs(z) * -LOG2E))


def _proj_kernel(x_ref, g_ref, w_ref, cos_ref, sin_ref,
                 rq_ref, rk_ref, rv_ref, gate_ref, sq_ref, sk_ref, sv_ref, *perm_refs, sb_transposed):
    x = x_ref[...]
    h = x * lax.rsqrt(jnp.mean(x * x, axis=-1, keepdims=True) + NORM_EPS) * g_ref[...]
    hb = h.astype(BF16)

    def seg(i):
        return jnp.dot(hb, w_ref[:, i * WIDTH:(i + 1) * WIDTH], preferred_element_type=F32)

    cos = jnp.concatenate([cos_ref[...]] * 4, axis=1)
    sin = jnp.concatenate([sin_ref[...]] * 4, axis=1)
    lane = lax.broadcasted_iota(jnp.int32, cos.shape, 1)
    first_half = (lane % HEAD_DIM) < (HEAD_DIM // 2)

    def rotary(p):
        swapped = jnp.where(first_half, pltpu.roll(p, WIDTH - HEAD_DIM // 2, 1), pltpu.roll(p, HEAD_DIM // 2, 1))
        return p * cos + swapped * sin

    rq_ref[...] = rotary(seg(0))
    rk_ref[...] = rotary(seg(1)) * QK_SCALE
    rv_ref[...] = seg(2)
    gate_ref[...] = seg(3)
    if sb_transposed:
        sq_ref[0] = (seg(4) * QK_SCALE).T.astype(BF16)
        k = seg(5)
        vt = seg(6).T
        sk_ref[0] = k.T
        sv_ref[0] = vt
        kp_ref, vpt_ref = perm_refs
        pi = lax.broadcasted_iota(jnp.int32, (SB_TILE, SB_TILE), 0)
        pk = lax.broadcasted_iota(jnp.int32, (SB_TILE, SB_TILE), 1)
        perm = jnp.where(pk == (pi % SB_GROUPS) * SB_GROUP_LEN + pi // SB_GROUPS, 1.0, 0.0).astype(BF16)
        kb = k.astype(BF16)
        vtb = vt.astype(BF16)
        for t in range(k.shape[0] // SB_TILE):
            sl = slice(t * SB_TILE, (t + 1) * SB_TILE)
            kp_ref[sl, :] = jnp.dot(perm, kb[sl], preferred_element_type=F32).astype(BF16)
            vpt_ref[0, :, sl] = lax.dot_general(vtb[:, sl], perm, (((1,), (1,)), ((), ())),
                                                preferred_element_type=F32).astype(BF16)
    else:
        sq_ref[...] = (seg(4) * QK_SCALE).astype(BF16)
        sk_ref[...] = seg(5)
        sv_ref[...] = seg(6)


def _proj(x, g, w_bf16, cos_tab, sin_tab, tm, n_seq, sb_transposed):
    t = x.shape[0]
    n_tab = cos_tab.shape[0] // tm
    tok = lambda i: (i, 0)
    tab = lambda i: (i % n_tab, 0)
    const = lambda i: (0, 0)
    out_f32 = jax.ShapeDtypeStruct((t, WIDTH), F32)
    blk = pl.BlockSpec((tm, WIDTH), tok)
    if sb_transposed:
        per_seq = t // (n_seq * tm)
        sb_blk = pl.BlockSpec((1, WIDTH, tm), lambda i: (i // per_seq, 0, i % per_seq))
        sb_shape = (n_seq, WIDTH, t // n_seq)
        perm_specs = [blk, sb_blk]
        perm_shapes = [jax.ShapeDtypeStruct((t, WIDTH), BF16), jax.ShapeDtypeStruct(sb_shape, BF16)]
    else:
        sb_blk = blk
        sb_shape = (t, WIDTH)
        perm_specs, perm_shapes = [], []
    return pl.pallas_call(
        functools.partial(_proj_kernel, sb_transposed=sb_transposed),
        grid=(t // tm,),
        in_specs=[pl.BlockSpec((tm, D_MODEL), tok),
                  pl.BlockSpec((1, D_MODEL), const),
                  pl.BlockSpec((D_MODEL, N_SEG * WIDTH), const),
                  pl.BlockSpec((tm, 128), tab),
                  pl.BlockSpec((tm, 128), tab)],
        out_specs=[blk] * 4 + [sb_blk] * 3 + perm_specs,
        out_shape=[out_f32] * 4 + [jax.ShapeDtypeStruct(sb_shape, BF16), jax.ShapeDtypeStruct(sb_shape, F32),
                                   jax.ShapeDtypeStruct(sb_shape, F32)] + perm_shapes,
        compiler_params=_params(("parallel",)),
        name="proj",
    )(x, g, w_bf16, cos_tab, sin_tab)


def _ret_kernel(q_ref, k_ref, v_ref, gate_ref, s0_ref, g_ref, o_ref, s_ref, decay_s, *, chunk):
    heads = range(N_HEADS)
    n_seq = q_ref.shape[0]
    sl = [slice(h * HEAD_DIM, (h + 1) * HEAD_DIM) for h in heads]

    @pl.when(pl.program_id(1) == 0)
    def _():
        s_ref[...] = s0_ref[...]
        row = lax.broadcasted_iota(jnp.int32, (chunk, chunk), 0)
        col = lax.broadcasted_iota(jnp.int32, (chunk, chunk), 1)
        diff = (row - col).astype(F32)
        for h in heads:
            decay_s[h] = jnp.where(diff >= 0, jnp.exp(_LOG_GAMMA[h] * jnp.maximum(diff, 0.0)), 0.0)

    idx = lax.broadcasted_iota(jnp.int32, (chunk, 1), 0).astype(F32)
    nt = (((1,), (1,)), ((), ()))
    tn = (((0,), (0,)), ((), ()))

    def sequences(seqs):
        units = [(s, h) for s in seqs for h in heads]
        qh = [q_ref[s, :, sl[h]].astype(BF16) for s, h in units]
        kf = [k_ref[s, :, sl[h]] for s, h in units]
        vh = [v_ref[s, :, sl[h]].astype(BF16) for s, h in units]
        state = [s_ref[s, h] for s, h in units]
        n = range(len(units))
        scores = [lax.dot_general(qh[i], kf[i].astype(BF16), nt, preferred_element_type=F32) for i in n]
        cross = [jnp.dot(qh[i], state[i].astype(BF16), preferred_element_type=F32) for i in n]
        kd = [(kf[i] * jnp.exp(_LOG_GAMMA[units[i][1]] * (chunk - 1.0 - idx))).astype(BF16) for i in n]
        kv = [lax.dot_general(kd[i], vh[i], tn, preferred_element_type=F32) for i in n]
        intra = [jnp.dot((scores[i] * decay_s[units[i][1]]).astype(BF16), vh[i], preferred_element_type=F32)
                 for i in n]
        for i, (s, h) in enumerate(units):
            lg = _LOG_GAMMA[h]
            s_ref[s, h] = jnp.exp(jnp.full((1, 1), lg * chunk, F32)) * state[i] + kv[i]
            o = intra[i] + cross[i] * jnp.exp(lg * (idx + 1.0))
            o = o - jnp.mean(o, axis=-1, keepdims=True)
            y = o * lax.rsqrt(jnp.mean(o * o, axis=-1, keepdims=True) + NORM_EPS)
            gate = gate_ref[s, :, sl[h]]
            o_ref[s, :, sl[h]] = y * g_ref[:, sl[h]] * (gate / (1.0 + jnp.exp(-gate)))

    if n_seq <= RET_UNROLL:
        for s in range(n_seq):
            sequences([s])
    else:
        def body(i, _):
            sequences([RET_INNER_BATCH * i + k for k in range(RET_INNER_BATCH)])
            return 0

        lax.fori_loop(0, n_seq // RET_INNER_BATCH, body, 0)


def _retention(rq, rk, rv, gate, state0, g, n_seq, chunk, seq_per_step):
    t = rq.shape[0]
    seq_len = t // n_seq
    tok = lambda b, c: (b, c, 0)
    st = lambda b, c: (b, 0, 0, 0)
    blk = pl.BlockSpec((seq_per_step, chunk, WIDTH), tok)
    st_blk = pl.BlockSpec((seq_per_step, N_HEADS, HEAD_DIM, HEAD_DIM), st)
    as_seq = lambda a: a.reshape(n_seq, seq_len, WIDTH)
    out, state = pl.pallas_call(
        functools.partial(_ret_kernel, chunk=chunk),
        grid=(n_seq // seq_per_step, seq_len // chunk),
        in_specs=[blk, blk, blk, blk, st_blk, pl.BlockSpec((1, WIDTH), lambda b, c: (0, 0))],
        out_specs=[blk, st_blk],
        out_shape=[jax.ShapeDtypeStruct((n_seq, seq_len, WIDTH), F32),
                   jax.ShapeDtypeStruct((n_seq, N_HEADS, HEAD_DIM, HEAD_DIM), F32)],
        scratch_shapes=[pltpu.VMEM((N_HEADS, chunk, chunk), F32)],
        compiler_params=_params(("arbitrary", "arbitrary")),
        name="retention",
    )(as_seq(rq), as_seq(rk), as_seq(rv), as_seq(gate), state0, g)
    return out.reshape(t, WIDTH), state


def _sb_kernel(pt_ref, bias_ref, qt_ref, kp_ref, vpt_ref, qbd_ref, bcol_ref, kn_ref, vn_ref, ck_ref, cv_ref,
               o_ref, os_ref,
               qm_s, s_s, w_s, a_s, acc_s, off_s, kbuf, vbuf, z_s, sp_s, sums_s, vb_s, carry_s, sacc_s, sem):
    n_tiles = qt_ref.shape[2] // SB_TILE
    rows = SB_HEADS * HEAD_DIM
    heads = range(SB_HEADS)
    group = pl.program_id(1)
    bias = [bias_ref[group * SB_HEADS + hh] for hh in heads]
    grp = lax.broadcasted_iota(jnp.int32, (SB_GROUPS, SB_TILE), 0)
    qry = lax.broadcasted_iota(jnp.int32, (SB_GROUPS, SB_TILE), 1)
    diag_bound = qry - SB_GROUP_LEN * grp

    rps, n_q, _ = qbd_ref.shape
    n_new = kn_ref.shape[1]
    n_pages = pt_ref.shape[1]
    upr = n_pages // SAMPLE_UNIT
    upr_shift = upr.bit_length() - 1
    n_units = rps * upr
    assert upr == 1 << upr_shift and n_tiles * (n_tiles + 1) // 2 >= n_units + 2
    req0 = (pl.program_id(0) * pl.num_programs(1) + group) * rps
    w = SAMPLE_TILE
    bcol = bcol_ref[...]
    nt = (((1,), (1,)), ((), ()))
    srow = lax.broadcasted_iota(jnp.int32, (w, 2 * w), 0)
    scol = lax.broadcasted_iota(jnp.int32, (w, 2 * w), 1)
    later = jnp.where((srow > scol) | (scol >= w), 1.0, 0.0).astype(BF16)

    def page_copies(u, slot):
        r = lax.shift_right_logical(u, upr_shift)
        first = n_pages - 1 - SAMPLE_UNIT * (u & (upr - 1))
        copies = []
        for i in range(SAMPLE_UNIT):
            page = pt_ref[req0 + r, first - i]
            copies.append(pltpu.make_async_copy(ck_ref.at[page], kbuf.at[slot, i], sem.at[slot]))
            copies.append(pltpu.make_async_copy(cv_ref.at[page], vbuf.at[slot, i], sem.at[slot]))
        return copies

    def tile_sums(z, mask):
        sp = _softplus(z)
        if mask is not None:
            sp = jnp.where(mask, sp, 0.0)
        lk = (-sp).astype(BF16)
        stacked = jnp.concatenate([lk[:, t * w:(t + 1) * w] for t in range(z.shape[1] // w)], axis=0)
        return sp, jnp.dot(stacked, later, preferred_element_type=F32)

    def weights(z, sp, sums, mask, carry):
        tiles = [None] * (z.shape[1] // w)
        for t in reversed(range(len(tiles))):
            s = sums[t * n_q:(t + 1) * n_q]
            tiles[t] = jnp.exp(z[:, t * w:(t + 1) * w] - sp[:, t * w:(t + 1) * w] + s[:, :w] + carry)
            carry = carry + s[:, w:]
        a = jnp.concatenate(tiles, axis=1)
        if mask is not None:
            a = jnp.where(mask, a, 0.0)
        pad_q = jnp.zeros((128 - n_q, a.shape[1]), F32)
        return jnp.concatenate([a, pad_q], axis=0).T.astype(BF16), carry

    def gather(buf, slot):
        return jnp.concatenate([buf[slot, i].reshape(WIDTH, PAGE_SIZE) for i in reversed(range(SAMPLE_UNIT))],
                               axis=1).astype(BF16)

    def last_stage_unit(u):
        return u - 2

    def sample_begin(u):
        @pl.when(u < n_units)
        def _():
            @pl.when(u + PAGE_PREFETCH < n_units)
            def _():
                for cp in page_copies(u + PAGE_PREFETCH, (u + PAGE_PREFETCH) & (RING - 1)):
                    cp.start()

            for cp in page_copies(u, u & (RING - 1)):
                cp.wait()

        uc = last_stage_unit(u)

        @pl.when((uc >= 0) & (uc < n_units) & ((uc & (upr - 1)) == 0))
        def _():
            r = lax.shift_right_logical(uc, upr_shift)
            pad = jnp.zeros((w - n_new, WIDTH), F32)
            kn = jnp.concatenate([kn_ref[r], pad], axis=0).astype(BF16)
            vnt = jnp.concatenate([vn_ref[r], pad], axis=0).T.astype(BF16)
            z = lax.dot_general(qbd_ref[r], kn, nt, preferred_element_type=F32) + bcol
            key = lax.broadcasted_iota(jnp.int32, (n_q, w), 1)
            tok = lax.broadcasted_iota(jnp.int32, (n_q, w), 0) % n_new
            mask = key < tok
            sp, sums = tile_sums(z, mask)
            at, carry = weights(z, sp, sums, mask, jnp.zeros((n_q, w), F32))
            carry_s[...] = carry
            sacc_s[...] = jnp.dot(vnt, at, preferred_element_type=F32)

    def sample_stages(u):
        r = jnp.minimum(lax.shift_right_logical(u, upr_shift), rps - 1)
        box = {}

        def scores_stage():
            ring = u & (RING - 1)
            z_s[ring] = jnp.dot(qbd_ref[r], gather(kbuf, ring), preferred_element_type=F32) + bcol

        def cast_stage():
            ring = u & (RING - 1)
            vb_s[ring] = gather(vbuf, ring)

        def sums_stage():
            ring = (u - 1) & (RING - 1)
            sp, sums = tile_sums(z_s[ring], None)
            sp_s[ring] = sp
            sums_s[ring] = sums

        def weights_stage():
            ring = (u - 2) & (RING - 1)
            old = carry_s[...]
            box["at"], carry = weights(z_s[ring], sp_s[ring], sums_s[ring], None, old)
            uc = jnp.full(old.shape, last_stage_unit(u), jnp.int32)
            carry_s[...] = jnp.where((uc >= 0) & (uc < n_units), carry, old)

        def values_stage(part):
            ring = (u - 2) & (RING - 1)
            rows_sl = pl.ds(part * (WIDTH // SB_HEADS), WIDTH // SB_HEADS)
            old = sacc_s[rows_sl, :]
            pv = jnp.dot(vb_s[ring, rows_sl, :], box["at"], preferred_element_type=F32)
            uc = jnp.full(old.shape, last_stage_unit(u), jnp.int32)
            sacc_s[rows_sl, :] = jnp.where((uc >= 0) & (uc < n_units), old + pv, old)

        return scores_stage, cast_stage, sums_stage, weights_stage, values_stage

    def sample_end(u):
        uc = last_stage_unit(u)

        @pl.when((uc >= 0) & (uc < n_units) & ((uc & (upr - 1)) == upr - 1))
        def _():
            r = lax.shift_right_logical(uc, upr_shift)
            acct = sacc_s[...].T
            os_ref[r] = jnp.concatenate(
                [acct[h * n_new:(h + 1) * n_new, h * HEAD_DIM:(h + 1) * HEAD_DIM] for h in range(N_HEADS)], axis=1)

    for first_unit in range(PAGE_PREFETCH):
        for cp in page_copies(first_unit, first_unit):
            cp.start()
    for ring_ref in (z_s, sp_s, sums_s, vb_s):
        ring_ref[...] = jnp.zeros_like(ring_ref)

    def tile_slice(c):
        return pl.ds(pl.multiple_of(c * SB_TILE, SB_TILE), SB_TILE)

    def scores(c, hh):
        s_s[hh] = jnp.dot(kp_ref[0, tile_slice(c), :], qm_s[hh], preferred_element_type=F32)

    def weighted_values(c, hh):
        return jnp.dot(vpt_ref[0, hh * HEAD_DIM:(hh + 1) * HEAD_DIM, tile_slice(c)], a_s[hh],
                       preferred_element_type=F32)

    def pass1(hh, carry, masked):
        run = jnp.zeros((SB_GROUPS, SB_TILE), F32)
        for j in reversed(range(SB_GROUP_LEN)):
            sl = slice(SB_GROUPS * j, SB_GROUPS * (j + 1))
            z = s_s[hh, sl, :] + bias[hh]
            sp = _softplus(z)
            if masked:
                sp = jnp.where(diag_bound > j, sp, 0.0)
            run = run - sp
            w_s[hh, sl, :] = z + run
        scan = run
        for sh in (1, 2, 4):
            scan = scan + jnp.where(grp + sh < SB_GROUPS, pltpu.roll(scan, SB_GROUPS - sh, 0), 0.0)
        offset = scan - run + carry
        return offset, jnp.broadcast_to(scan[0:1, :], carry.shape) + carry

    def pass2(hh, offset, masked):
        for m in range(SB_GROUP_LEN // 2):
            pair = []
            for j in (2 * m, 2 * m + 1):
                a = jnp.exp(w_s[hh, SB_GROUPS * j:SB_GROUPS * (j + 1), :] + offset)
                if masked:
                    a = jnp.where(diag_bound > j, a, 0.0)
                pair.append(a)
            a_s[hh, 2 * SB_GROUPS * m:2 * SB_GROUPS * (m + 1), :] = jnp.concatenate(pair, axis=0).astype(BF16)

    def tile_iteration(u, nxt, carries, masked):
        scores_stage, cast_stage, sums_stage, weights_stage, values_stage = sample_stages(u)
        weights_stage()
        state = []
        for hh in heads:
            values_stage(hh)
            state.append(pass1(hh, carries[hh], masked))
            scores(nxt, hh)
        for hh in heads:
            off_s[hh] = state[hh][0]

        @pl.when(u >= 0)
        def _():
            sums_stage()
            scores_stage()
            cast_stage()
            for hh in heads:
                pass2(hh, off_s[hh], masked)

        return tuple(st[1] for st in state)

    head_of_row = lax.broadcasted_iota(jnp.int32, (rows, SB_TILE), 0) // HEAD_DIM

    def qblock(qb, u):
        qt = qt_ref[0, :, pl.ds(pl.multiple_of(qb * SB_TILE, SB_TILE), SB_TILE)]
        for hh in heads:
            qm_s[hh] = jnp.where(head_of_row == hh, qt, jnp.zeros_like(qt))
        acc_s[...] = jnp.zeros_like(acc_s)
        sample_begin(u)
        for hh in heads:
            scores(qb, hh)
        zero_c = jnp.zeros((SB_GROUPS, SB_TILE), F32)
        carries = tile_iteration(u, jnp.maximum(qb - 1, 0), (zero_c,) * SB_HEADS, True)
        sample_end(u)

        def body(i, cu):
            carries, u = cu
            c = qb - 1 - i
            sample_begin(u)
            pv = [weighted_values(c + 1, hh) for hh in heads]
            carries = tile_iteration(u, jnp.maximum(c - 1, 0), carries, False)
            for hh in heads:
                acc_s[hh] += pv[hh]
            sample_end(u)
            return carries, u + 1

        _, u = lax.fori_loop(0, qb, body, (carries, u + 1))
        for hh in heads:
            o_ref[0, hh, qb] = acc_s[hh] + weighted_values(0, hh)
        return u

    lax.fori_loop(0, n_tiles, qblock, jnp.int32(0))


def _sb(bias, qt, kp, vpt, page_table, qbd, bias_col, k_new, v_new, cache_k, cache_v):
    b, _, seq = qt.shape
    n_tiles = seq // SB_TILE
    rows = SB_HEADS * HEAD_DIM
    groups = N_HEADS // SB_HEADS
    n_req, n_q, _ = qbd.shape
    n_new = k_new.shape[1]
    rps = n_req // (b * groups)
    page_shape = (RING, SAMPLE_UNIT, N_HEADS, HEAD_DIM, PAGE_SIZE)
    unit_keys = SAMPLE_UNIT * PAGE_SIZE
    in_blk = pl.BlockSpec((1, rows, seq), lambda i, g, pt: (i, g, 0))
    key_blk = pl.BlockSpec((1, seq, rows), lambda i, g, pt: (i, 0, g))
    req_map = lambda i, g, pt: (i * groups + g, 0, 0)
    grid_spec = pltpu.PrefetchScalarGridSpec(
        num_scalar_prefetch=1,
        grid=(b, groups),
        in_specs=[pl.BlockSpec(memory_space=pltpu.SMEM), in_blk, key_blk, in_blk,
                  pl.BlockSpec((rps, n_q, WIDTH), req_map),
                  pl.BlockSpec((n_q, 1), lambda i, g, pt: (0, 0)),
                  pl.BlockSpec((rps, n_new, WIDTH), req_map),
                  pl.BlockSpec((rps, n_new, WIDTH), req_map),
                  pl.BlockSpec(memory_space=pl.ANY),
                  pl.BlockSpec(memory_space=pl.ANY)],
        out_specs=[pl.BlockSpec((1, SB_HEADS, n_tiles, HEAD_DIM, SB_TILE), lambda i, g, pt: (i, g, 0, 0, 0)),
                   pl.BlockSpec((rps, n_new, WIDTH), req_map)],
        scratch_shapes=[pltpu.VMEM((SB_HEADS, rows, SB_TILE), BF16),
                        pltpu.VMEM((SB_HEADS, SB_TILE, SB_TILE), F32),
                        pltpu.VMEM((SB_HEADS, SB_TILE, SB_TILE), F32),
                        pltpu.VMEM((SB_HEADS, SB_TILE, SB_TILE), BF16),
                        pltpu.VMEM((SB_HEADS, HEAD_DIM, SB_TILE), F32),
                        pltpu.VMEM((SB_HEADS, SB_GROUPS, SB_TILE), F32),
                        pltpu.VMEM(page_shape, F32),
                        pltpu.VMEM(page_shape, F32),
                        pltpu.VMEM((RING, n_q, unit_keys), F32),
                        pltpu.VMEM((RING, n_q, unit_keys), F32),
                        pltpu.VMEM((RING, n_q * unit_keys // SAMPLE_TILE, 2 * SAMPLE_TILE), F32),
                        pltpu.VMEM((RING, WIDTH, unit_keys), BF16),
                        pltpu.VMEM((n_q, SAMPLE_TILE), F32),
                        pltpu.VMEM((WIDTH, 128), F32),
                        pltpu.SemaphoreType.DMA((RING,))],
    )
    return pl.pallas_call(
        _sb_kernel,
        grid_spec=grid_spec,
        out_shape=[jax.ShapeDtypeStruct((b, N_HEADS, n_tiles, HEAD_DIM, SB_TILE), F32),
                   jax.ShapeDtypeStruct((n_req, n_new, WIDTH), F32)],
        compiler_params=_params(("arbitrary", "arbitrary")),
        name="sb",
    )(page_table, bias, qt, kp, vpt, qbd, bias_col, k_new, v_new, cache_k, cache_v)


def _out_mlp_kernel(x_ref, ret_ref, sb_ref, sbg_ref, wo_ref, g_post_ref, g_pre_ref, g_mlp_ref,
                    wu_ref, wd_ref, y_ref, *, sb_transposed):
    def rms(v, g_ref):
        return v * lax.rsqrt(jnp.mean(v * v, axis=-1, keepdims=True) + NORM_EPS) * g_ref[...]

    mix = jnp.dot(ret_ref[...].astype(BF16), wo_ref[:WIDTH, :], preferred_element_type=F32)
    if sb_transposed:
        o = sb_ref[0, :, 0]
        y = o * lax.rsqrt(jnp.mean(o * o, axis=1, keepdims=True) + NORM_EPS) * sbg_ref[...]
        sb = y.reshape(WIDTH, y.shape[-1]).T.astype(BF16)
    else:
        parts = []
        for h in range(N_HEADS):
            sl = slice(h * HEAD_DIM, (h + 1) * HEAD_DIM)
            o = sb_ref[:, sl]
            parts.append(o * lax.rsqrt(jnp.mean(o * o, axis=-1, keepdims=True) + NORM_EPS) * sbg_ref[:, sl])
        sb = jnp.concatenate(parts, axis=1).astype(BF16)
    mix = mix + jnp.dot(sb, wo_ref[WIDTH:, :], preferred_element_type=F32)
    y1 = x_ref[...] + rms(mix, g_post_ref)
    h2 = rms(y1, g_pre_ref).astype(BF16)
    u = jnp.maximum(jnp.dot(h2, wu_ref[...], preferred_element_type=F32), 0.0)
    d = jnp.dot((u * u).astype(BF16), wd_ref[...], preferred_element_type=F32)
    y_ref[...] = y1 + rms(d, g_mlp_ref)


def _out_mlp(x, ret, sb, sb_g, w_out, g_post, g_pre, g_mlp, w_up, w_down, tm, sb_transposed):
    t = x.shape[0]
    tok = lambda i: (i, 0)
    const = lambda i: (0, 0)
    if sb_transposed:
        n_tiles = sb.shape[2]
        sb_spec = pl.BlockSpec((1, N_HEADS, 1, HEAD_DIM, tm), lambda i: (i // n_tiles, 0, i % n_tiles, 0, 0))
        sbg_spec = pl.BlockSpec((N_HEADS, HEAD_DIM, 1), lambda i: (0, 0, 0))
    else:
        sb_spec = pl.BlockSpec((tm, WIDTH), tok)
        sbg_spec = pl.BlockSpec((1, WIDTH), const)
    vec = pl.BlockSpec((1, D_MODEL), const)
    return pl.pallas_call(
        functools.partial(_out_mlp_kernel, sb_transposed=sb_transposed),
        grid=(t // tm,),
        in_specs=[pl.BlockSpec((tm, D_MODEL), tok),
                  pl.BlockSpec((tm, WIDTH), tok),
                  sb_spec, sbg_spec,
                  pl.BlockSpec((2 * WIDTH, D_MODEL), const, pipeline_mode=pl.Buffered(1)),
                  vec, vec, vec,
                  pl.BlockSpec((D_MODEL, D_FF), const, pipeline_mode=pl.Buffered(1)),
                  pl.BlockSpec((D_FF, D_MODEL), const, pipeline_mode=pl.Buffered(1))],
        out_specs=pl.BlockSpec((tm, D_MODEL), tok),
        out_shape=jax.ShapeDtypeStruct((t, D_MODEL), F32),
        compiler_params=_params(("parallel",)),
        name="out_mlp",
    )(x, ret, sb, sb_g, w_out, g_post, g_pre, g_mlp, w_up, w_down)


def _rotary_tables(pos):
    half = HEAD_DIM // 2
    inv = ROPE_BASE ** (-jnp.arange(half, dtype=F32) / half)
    ang = pos[:, None] * inv[None, :]
    cos, sin = jnp.cos(ang), jnp.sin(ang)
    return jnp.tile(jnp.concatenate([cos, cos], axis=1), (1, 2)), jnp.tile(jnp.concatenate([-sin, sin], axis=1), (1, 2))


def kernel(x_prompt, x_sample, cache_sb_k, cache_sb_v, page_table, state_ret, norm_mix_pre, norm_mix_post, w_in, ret_norm_g, sb_bias, sb_norm_g, w_out, norm_mlp_pre, norm_mlp_post, w_up, w_down):
    assert w_in.shape[0] == 1, "single layer"
    batch, seq, _ = x_prompt.shape
    n_req, n_new, _ = x_sample.shape
    n_pages = page_table.shape[1]
    past_len = n_pages * PAGE_SIZE

    w_in_b = w_in[0].astype(BF16)
    w_out_b = w_out[0].astype(BF16)
    w_up_b = w_up[0].astype(BF16)
    w_down_b = w_down[0].astype(BF16)
    bias = sb_bias[0].astype(F32)
    dt = x_prompt.dtype

    tm_p = 512
    cos_p, sin_p = _rotary_tables(jnp.arange(seq, dtype=F32))
    xp = x_prompt.reshape(batch * seq, D_MODEL)
    rq, rk, rv, gate, sqt, skt, svt, kp, vpt = _proj(xp, norm_mix_pre, w_in_b, cos_p, sin_p, tm_p, batch, True)
    zero_state = jnp.zeros((batch, N_HEADS, HEAD_DIM, HEAD_DIM), F32)
    ret_p, state_p = _retention(rq, rk, rv, gate, zero_state, ret_norm_g, batch, RET_CHUNK, 1)

    tm_s = 256
    cos_s, sin_s = _rotary_tables(past_len + jnp.arange(n_new, dtype=F32))
    cos_s, sin_s = jnp.tile(cos_s, (tm_s // n_new, 1)), jnp.tile(sin_s, (tm_s // n_new, 1))
    xs = x_sample.reshape(n_req * n_new, D_MODEL)
    rq, rk, rv, gate, sq, sk_s, sv_s = _proj(xs, norm_mix_pre, w_in_b, cos_s, sin_s, tm_s, n_req, False)
    ret_s, state_s = _retention(rq, rk, rv, gate, state_ret[0].astype(F32), ret_norm_g, n_req, n_new, RET_SAMPLE_BATCH)

    q4 = sq.reshape(n_req, n_new, N_HEADS, HEAD_DIM).transpose(0, 2, 1, 3)
    eye = jnp.eye(N_HEADS, dtype=BF16)
    qbd = (q4[:, :, :, None, :] * eye[None, :, None, :, None]).reshape(n_req, N_HEADS * n_new, WIDTH)
    bias_col = jnp.repeat(bias, n_new)[:, None]
    cache_k = cache_sb_k[0].transpose(0, 2, 3, 1)
    cache_v = cache_sb_v[0].transpose(0, 2, 3, 1)
    sb_p, sb_s = _sb(bias, sqt, kp.reshape(batch, seq, WIDTH), vpt, page_table, qbd, bias_col,
                     sk_s.reshape(n_req, n_new, WIDTH),
                     sv_s.reshape(n_req, n_new, WIDTH), cache_k, cache_v)

    y_p = _out_mlp(xp, ret_p, sb_p, sb_norm_g.reshape(N_HEADS, HEAD_DIM, 1), w_out_b, norm_mix_post,
                   norm_mlp_pre, norm_mlp_post, w_up_b, w_down_b, SB_TILE, True)
    y_s = _out_mlp(xs, ret_s, sb_s.reshape(n_req * n_new, WIDTH), sb_norm_g, w_out_b, norm_mix_post,
                   norm_mlp_pre, norm_mlp_post, w_up_b, w_down_b, tm_s, False)
    k_prompt = skt.reshape(1, batch, N_HEADS, HEAD_DIM, seq).transpose(0, 1, 4, 2, 3)
    v_prompt = svt.reshape(1, batch, N_HEADS, HEAD_DIM, seq).transpose(0, 1, 4, 2, 3)

    return (y_p.reshape(batch, seq, D_MODEL).astype(dt),
            y_s.reshape(n_req, n_new, D_MODEL).astype(dt),
            k_prompt.astype(dt),
            v_prompt.astype(dt),
            state_p[None].astype(dt),
            sk_s.reshape(1, n_req, n_new, N_HEADS, HEAD_DIM).astype(cache_sb_k.dtype),
            sv_s.reshape(1, n_req, n_new, N_HEADS, HEAD_DIM).astype(cache_sb_v.dtype),
            state_s[None].astype(state_ret.dtype))
```
